```python
import math
import jax, jax.numpy as jnp
from jax import lax
import numpy as np

D_MODEL = 1024
BATCH = 2
SEQ = 8192
DEPTH = 4

D_MIX = 2 * D_MODEL
D_BRANCH = D_MIX // 4
CONV_W = 4
EPS = 1e-6

LRU_HEADS = 8
LRU_HDIM = D_BRANCH // LRU_HEADS
LRU_C = 8.0
S5_GROUP = 16
S5_GROUPS = D_BRANCH // S5_GROUP
S5_STATE = 64
GLA_HEADS = 4
GLA_DK = D_BRANCH // 2
GLA_HK = GLA_DK // GLA_HEADS
GLA_HV = D_BRANCH // GLA_HEADS
GLA_RANK = 16
GLA_TAU = 16.0
GLA_CHUNK = 64
SSD_HDIM = 64
SSD_HEADS = D_BRANCH // SSD_HDIM
SSD_GROUPS = 2
SSD_HPG = SSD_HEADS // SSD_GROUPS
SSD_STATE = 128
SSD_CHUNK = 64
SSD_CONV_DIM = D_BRANCH + 2 * SSD_GROUPS * SSD_STATE

IN_SIZES = (
    D_BRANCH, D_BRANCH,
    D_BRANCH, D_BRANCH,
    GLA_DK, GLA_DK, D_BRANCH, D_BRANCH, GLA_RANK,
    D_BRANCH, SSD_CONV_DIM, SSD_HEADS,
)
D_IN = sum(IN_SIZES)

kernel_name = "hybrid_parallel_rglru_s5_gla_ssd"

F32 = jnp.float32


def rmsnorm(x, w):
    xf = x.astype(F32)
    y = xf * lax.rsqrt(jnp.mean(xf * xf, axis=-1, keepdims=True) + EPS)
    return (y * w.astype(F32)).astype(x.dtype)


def causal_dwconv(x, w, b):
    c = x.shape[-1]
    y = lax.conv_general_dilated(x, w[:, None, :].astype(x.dtype), window_strides=(1,),
                                 padding=[(CONV_W - 1, 0)],
                                 dimension_numbers=("NWC", "WIO", "NWC"),
                                 feature_group_count=c)
    return y + b.astype(x.dtype)


def _linear_combine(e1, e2):
    a1, b1 = e1
    a2, b2 = e2
    return a1 * a2, a2 * b1 + b2


def _complex_linear_combine(e1, e2):
    a1r, a1i, b1r, b1i = e1
    a2r, a2i, b2r, b2i = e2
    return (a2r * a1r - a2i * a1i, a2r * a1i + a2i * a1r,
            a2r * b1r - a2i * b1i + b2r, a2r * b1i + a2i * b1r + b2i)


def rglru_mixer(xa, za, conv_w, conv_b, w_r, b_r, w_i, b_i, lru_l):
    bsz, s, _ = xa.shape
    u = causal_dwconv(xa, conv_w, conv_b).astype(F32)
    uh = u.reshape(bsz, s, LRU_HEADS, LRU_HDIM)
    r = jax.nn.sigmoid(jnp.einsum("bshi,hij->bshj", uh, w_r.astype(F32)).reshape(bsz, s, D_BRANCH) + b_r.astype(F32))
    i = jax.nn.sigmoid(jnp.einsum("bshi,hij->bshj", uh, w_i.astype(F32)).reshape(bsz, s, D_BRANCH) + b_i.astype(F32))
    log_a = -LRU_C * r * jax.nn.softplus(-lru_l.astype(F32))
    a = jnp.exp(log_a)
    mult = jnp.sqrt(-jnp.expm1(2.0 * log_a))
    _, h = lax.associative_scan(_linear_combine, (a, mult * i * u), axis=1)
    return h * jax.nn.silu(za.astype(F32))


def s5_mixer(ub, zb, lam_re, lam_im, log_dt, b_re, b_im, c_re, c_im, d_skip, glu_w, glu_b):
    bsz, s, _ = ub.shape
    u = ub.astype(F32)
    ug = u.reshape(bsz, s, S5_GROUPS, S5_GROUP)
    lr = lam_re.astype(F32)
    li = lam_im.astype(F32)
    dt = jnp.exp(log_dt.astype(F32))[:, None]
    mag = jnp.exp(lr * dt)
    ab_re = mag * jnp.cos(li * dt)
    ab_im = mag * jnp.sin(li * dt)
    den = lr * lr + li * li
    nr = ab_re - 1.0
    coef_re = (nr * lr + ab_im * li) / den
    coef_im = (ab_im * lr - nr * li) / den
    br = b_re.astype(F32)
    bi = b_im.astype(F32)
    bb_re = coef_re[..., None] * br - coef_im[..., None] * bi
    bb_im = coef_re[..., None] * bi + coef_im[..., None] * br
    bu_re = jnp.einsum("bsgp,gnp->bsgn", ug, bb_re)
    bu_im = jnp.einsum("bsgp,gnp->bsgn", ug, bb_im)
    a_re = jnp.broadcast_to(ab_re, bu_re.shape)
    a_im = jnp.broadcast_to(ab_im, bu_re.shape)
    _, _, st_re, st_im = lax.associative_scan(_complex_linear_combine, (a_re, a_im, bu_re, bu_im), axis=1)
    y = (jnp.einsum("bsgn,gpn->bsgp", st_re, c_re.astype(F32))
         - jnp.einsum("bsgn,gpn->bsgp", st_im, c_im.astype(F32)))
    y = y.reshape(bsz, s, D_BRANCH) + d_skip.astype(F32) * u
    y = jax.nn.gelu(y)
    y = y * jax.nn.sigmoid(y @ glu_w.astype(F32) + glu_b.astype(F32))
    return y * jax.nn.silu(zb.astype(F32))


def gla_mixer(q, k, v, zc, g_low, w_gate, b_gate, norm_w):
    bsz, s, _ = q.shape
    nc = s // GLA_CHUNK
    shp = (bsz, nc, GLA_CHUNK, GLA_HEADS, -1)
    q = q.astype(F32).reshape(shp) * (GLA_HK ** -0.5)
    k = k.astype(F32).reshape(shp)
    v = v.astype(F32).reshape(shp)
    logits = g_low.astype(F32) @ w_gate.astype(F32) + b_gate.astype(F32)
    g = (jax.nn.log_sigmoid(logits) / GLA_TAU).reshape(shp)
    gc = jnp.cumsum(g, axis=2)
    g_last = gc[:, :, -1:]
    q_dec = q * jnp.exp(gc)
    k_inv = k * jnp.exp(-gc)
    k_end = k * jnp.exp(g_last - gc)
    mask = jnp.tril(jnp.ones((GLA_CHUNK, GLA_CHUNK), dtype=bool))
    scores = jnp.where(mask, jnp.einsum("bnihd,bnjhd->bnhij", q_dec, k_inv), 0.0)
    o_intra = jnp.einsum("bnhij,bnjhe->bnihe", scores, v)
    chunk_kv = jnp.einsum("bnjhd,bnjhe->bnhde", k_end, v)
    decay = jnp.exp(g_last[:, :, 0])

    def step(state, inp):
        dec, kv = inp
        return dec[..., None] * state + kv, state

    init = jnp.zeros((bsz, GLA_HEADS, GLA_HK, GLA_HV), F32)
    _, prev = lax.scan(step, init, (jnp.moveaxis(decay, 1, 0), jnp.moveaxis(chunk_kv, 1, 0)))
    prev = jnp.moveaxis(prev, 0, 1)
    o = o_intra + jnp.einsum("bnihd,bnhde->bnihe", q_dec, prev)
    o = o * lax.rsqrt(jnp.mean(o * o, axis=-1, keepdims=True) + EPS) * norm_w.astype(F32)
    o = o.reshape(bsz, s, D_BRANCH)
    return o * jax.nn.silu(zc.astype(F32))


def ssd_mixer(zd, xbc, dt_raw, conv_w, conv_b, dt_bias, a_log, d_skip, norm_w):
    bsz, s, _ = zd.shape
    nc = s // SSD_CHUNK
    xbc = jax.nn.silu(causal_dwconv(xbc, conv_w, conv_b).astype(F32))
    xs, bm, cm = jnp.split(xbc, [D_BRANCH, D_BRANCH + SSD_GROUPS * SSD_STATE], axis=-1)
    x = xs.reshape(bsz, nc, SSD_CHUNK, SSD_GROUPS, SSD_HPG, SSD_HDIM)
    bm = bm.reshape(bsz, nc, SSD_CHUNK, SSD_GROUPS, SSD_STATE)
    cm = cm.reshape(bsz, nc, SSD_CHUNK, SSD_GROUPS, SSD_STATE)
    dt = jax.nn.softplus(dt_raw.astype(F32) + dt_bias.astype(F32))
    a = -jnp.exp(a_log.astype(F32))
    dt_t = jnp.moveaxis((dt).reshape(bsz, nc, SSD_CHUNK, SSD_GROUPS, SSD_HPG), 2, -1)
    a_cum = jnp.cumsum(jnp.moveaxis((dt * a).reshape(bsz, nc, SSD_CHUNK, SSD_GROUPS, SSD_HPG), 2, -1), axis=-1)
    mask = jnp.tril(jnp.ones((SSD_CHUNK, SSD_CHUNK), dtype=bool))
    diff = a_cum[..., :, None] - a_cum[..., None, :]
    seg = jnp.where(mask, jnp.exp(jnp.where(mask, diff, 0.0)), 0.0)
    cb = jnp.einsum("bnigs,bnjgs->bngij", cm, bm)
    m = cb[:, :, :, None] * seg * dt_t[..., None, :]
    y_diag = jnp.einsum("bnghij,bnjghp->bnighp", m, x)
    decay_end = jnp.exp(a_cum[..., -1:] - a_cum) * dt_t
    states = jnp.einsum("bnjgs,bnghj,bnjghp->bnghps", bm, decay_end, x)
    chunk_decay = jnp.exp(a_cum[..., -1])

    def step(state, inp):
        dec, st = inp
        return dec[..., None, None] * state + st, state

    init = jnp.zeros((bsz, SSD_GROUPS, SSD_HPG, SSD_HDIM, SSD_STATE), F32)
    _, prev = lax.scan(step, init, (jnp.moveaxis(chunk_decay, 1, 0), jnp.moveaxis(states, 1, 0)))
    prev = jnp.moveaxis(prev, 0, 1)
    y_off = jnp.einsum("bnigs,bnghps,bnghi->bnighp", cm, prev, jnp.exp(a_cum))
    y = y_diag + y_off + d_skip.astype(F32).reshape(SSD_GROUPS, SSD_HPG)[:, :, None] * x
    y = y.reshape(bsz, s, D_BRANCH) * jax.nn.silu(zd.astype(F32))
    y = y * lax.rsqrt(jnp.mean(y * y, axis=-1, keepdims=True) + EPS) * norm_w.astype(F32)
    return y


def setup_inputs(seed: int = 0) -> dict:
    key = jax.random.key(seed)
    ks = jax.random.split(key, 40)
    nrm = lambda k, shape, scale: jax.random.normal(k, shape, F32) * scale
    uni = lambda k, shape, lo, hi: jax.random.uniform(k, shape, F32, lo, hi)
    L = DEPTH
    x = jax.random.normal(ks[0], (BATCH, SEQ, D_MODEL), F32)
    norm_w = 1.0 + nrm(ks[1], (L, D_MODEL), 0.02)
    w_in = nrm(ks[2], (L, D_MODEL, D_IN), D_MODEL ** -0.5)
    lru_conv_w = nrm(ks[3], (L, CONV_W, D_BRANCH), CONV_W ** -0.5)
    lru_conv_b = nrm(ks[4], (L, D_BRANCH), 0.02)
    lru_w_r = nrm(ks[5], (L, LRU_HEADS, LRU_HDIM, LRU_HDIM), LRU_HDIM ** -0.5)
    lru_b_r = nrm(ks[6], (L, D_BRANCH), 0.02)
    lru_w_i = nrm(ks[7], (L, LRU_HEADS, LRU_HDIM, LRU_HDIM), LRU_HDIM ** -0.5)
    lru_b_i = nrm(ks[8], (L, D_BRANCH), 0.02)
    p = uni(ks[9], (L, D_BRANCH), 0.9, 0.999) ** (1.0 / LRU_C)
    lru_l = jnp.log(p) - jnp.log1p(-p)
    n_idx = jnp.arange(S5_STATE, dtype=F32)
    s5_lam_re = -0.5 + nrm(ks[10], (L, S5_GROUPS, S5_STATE), 0.01)
    s5_lam_im = math.pi * n_idx + nrm(ks[11], (L, S5_GROUPS, S5_STATE), 0.01)
    s5_log_dt = uni(ks[12], (L, S5_GROUPS), math.log(1e-3), math.log(1e-1))
    s5_b_re = nrm(ks[13], (L, S5_GROUPS, S5_STATE, S5_GROUP), (2.0 * S5_GROUP) ** -0.5)
    s5_b_im = nrm(ks[14], (L, S5_GROUPS, S5_STATE, S5_GROUP), (2.0 * S5_GROUP) ** -0.5)
    s5_c_re = nrm(ks[15], (L, S5_GROUPS, S5_GROUP, S5_STATE), (2.0 * S5_STATE) ** -0.5)
    s5_c_im = nrm(ks[16], (L, S5_GROUPS, S5_GROUP, S5_STATE), (2.0 * S5_STATE) ** -0.5)
    s5_d = nrm(ks[17], (L, D_BRANCH), 1.0)
    s5_glu_w = nrm(ks[18], (L, D_BRANCH, D_BRANCH), D_BRANCH ** -0.5)
    s5_glu_b = nrm(ks[19], (L, D_BRANCH), 0.02)
    gla_w_gate = nrm(ks[20], (L, GLA_RANK, GLA_DK), GLA_RANK ** -0.5)
    gla_b_gate = nrm(ks[21], (L, GLA_DK), 0.1)
    gla_norm_w = 1.0 + nrm(ks[22], (L, GLA_HV), 0.02)
    ssd_conv_w = nrm(ks[23], (L, CONV_W, SSD_CONV_DIM), CONV_W ** -0.5)
    ssd_conv_b = nrm(ks[24], (L, SSD_CONV_DIM), 0.02)
    dt0 = jnp.exp(uni(ks[25], (L, SSD_HEADS), math.log(1e-3), math.log(1e-1)))
    ssd_dt_bias = dt0 + jnp.log(-jnp.expm1(-dt0))
    ssd_a_log = jnp.log(uni(ks[26], (L, SSD_HEADS), 1.0, 16.0))
    ssd_d = 1.0 + nrm(ks[27], (L, SSD_HEADS), 0.1)
    ssd_norm_w = 1.0 + nrm(ks[28], (L, D_BRANCH), 0.02)
    w_out = nrm(ks[29], (L, D_MIX, D_MODEL), D_MIX ** -0.5)
    norm_f_w = 1.0 + nrm(ks[30], (D_MODEL,), 0.02)
    return {
        "x": x, "norm_w": norm_w, "w_in": w_in,
        "lru_conv_w": lru_conv_w, "lru_conv_b": lru_conv_b, "lru_w_r": lru_w_r, "lru_b_r": lru_b_r,
        "lru_w_i": lru_w_i, "lru_b_i": lru_b_i, "lru_l": lru_l,
        "s5_lam_re": s5_lam_re, "s5_lam_im": s5_lam_im, "s5_log_dt": s5_log_dt,
        "s5_b_re": s5_b_re, "s5_b_im": s5_b_im, "s5_c_re": s5_c_re, "s5_c_im": s5_c_im,
        "s5_d": s5_d, "s5_glu_w": s5_glu_w, "s5_glu_b": s5_glu_b,
        "gla_w_gate": gla_w_gate, "gla_b_gate": gla_b_gate, "gla_norm_w": gla_norm_w,
        "ssd_conv_w": ssd_conv_w, "ssd_conv_b": ssd_conv_b, "ssd_dt_bias": ssd_dt_bias,
        "ssd_a_log": ssd_a_log, "ssd_d": ssd_d, "ssd_norm_w": ssd_norm_w,
        "w_out": w_out, "norm_f_w": norm_f_w,
    }


def reference(x, norm_w, w_in, lru_conv_w, lru_conv_b, lru_w_r, lru_b_r, lru_w_i, lru_b_i, lru_l,
              s5_lam_re, s5_lam_im, s5_log_dt, s5_b_re, s5_b_im, s5_c_re, s5_c_im, s5_d, s5_glu_w, s5_glu_b,
              gla_w_gate, gla_b_gate, gla_norm_w,
              ssd_conv_w, ssd_conv_b, ssd_dt_bias, ssd_a_log, ssd_d, ssd_norm_w,
              w_out, norm_f_w):
    split_idx = np.cumsum(IN_SIZES)[:-1].tolist()
    for l in range(DEPTH):
        h = rmsnorm(x, norm_w[l])
        proj = h @ w_in[l]
        (a_x, a_z, b_u, b_z, c_q, c_k, c_v, c_z, c_g, d_z, d_xbc, d_dt) = jnp.split(proj, split_idx, axis=-1)
        y_a = rglru_mixer(a_x, a_z, lru_conv_w[l], lru_conv_b[l], lru_w_r[l], lru_b_r[l],
                          lru_w_i[l], lru_b_i[l], lru_l[l])
        y_b = s5_mixer(b_u, b_z, s5_lam_re[l], s5_lam_im[l], s5_log_dt[l], s5_b_re[l], s5_b_im[l],
                       s5_c_re[l], s5_c_im[l], s5_d[l], s5_glu_w[l], s5_glu_b[l])
        y_c = gla_mixer(c_q, c_k, c_v, c_z, c_g, gla_w_gate[l], gla_b_gate[l], gla_norm_w[l])
        y_d = ssd_mixer(d_z, d_xbc, d_dt, ssd_conv_w[l], ssd_conv_b[l], ssd_dt_bias[l],
                        ssd_a_log[l], ssd_d[l], ssd_norm_w[l])
        y = jnp.concatenate([y_a, y_b, y_c, y_d], axis=-1).astype(x.dtype)
        x = x + y @ w_out[l]
    return rmsnorm(x, norm_f_w)
```

```python
import functools
import math

import jax
import jax.numpy as jnp
from jax import lax
from jax.experimental import pallas as pl
from jax.experimental.pallas import tpu as pltpu

F32 = jnp.float32
BF16 = jnp.bfloat16

D_MODEL = 1024
D_BRANCH = 512
CONV_W = 4
EPS = 1e-6
LRU_HEADS = 8
LRU_HDIM = 64
LRU_C = 8.0
S5_GROUP = 16
S5_GROUPS = 32
S5_STATE = 64
GLA_HEADS = 4
GLA_DK = 256
GLA_HK = 64
GLA_HV = 128
GLA_RANK = 16
GLA_TAU = 16.0
SSD_HEADS = 8
SSD_HDIM = 64
SSD_GROUPS = 2
SSD_STATE = 128
CHUNK = 64

LANES = 128
SUBLANES = 8
VMEM_LIMIT_BYTES = 60 * 1024 * 1024

TILE = 512
NCHUNK = TILE // CHUNK
S5_R = 4
S5_PH = 8
S5_NB = TILE // S5_R
S5_J = S5_NB // S5_PH
S5_SLABS = D_BRANCH // LANES
S5_GPS = LANES // S5_GROUP
S5_SW = S5_GPS * S5_STATE
S5_HS_STEPS = tuple(k for k in (1, 2, 4, 8, 16, 32) if k < S5_J)
PW_L1 = 0
PW_HS = S5_PH - 1
PW_CARRY = PW_HS + len(S5_HS_STEPS)
PW_ROWS = PW_CARRY + S5_J

SEG_A = 0
SEG_B = 1024
SEG_C = 2048
SEG_C_W = 1664
SEG_D = SEG_C + SEG_C_W
SEG_D_W = 1664
N_IN = SEG_D + SEG_D_W
ORIG_C_END = 3600
ORIG_D_IN = 5144


def _dot(a, b):
    return jnp.dot(a, b, preferred_element_type=F32)


def _dot_nt(a, b):
    return lax.dot_general(a, b, (((1,), (1,)), ((), ())), preferred_element_type=F32)


def _dot_tn(a, b):
    return lax.dot_general(a, b, (((0,), (0,)), ((), ())), preferred_element_type=F32)


def _bf(x):
    return x.astype(BF16)


def _softplus(x):
    return jnp.maximum(x, 0.0) + jnp.log1p(jnp.exp(-jnp.abs(x)))


def _sigmoid(x):
    return 1.0 / (1.0 + jnp.exp(-x))


def _silu(x):
    return x * _sigmoid(x)


def _gelu_tanh(x):
    c = math.sqrt(2.0 / math.pi)
    return 0.5 * x * (1.0 + jnp.tanh(c * (x + 0.044715 * (x * x * x))))


def _cmul(ar, ai, br, bi):
    return ar * br - ai * bi, ar * bi + ai * br


def _split_hi_lo(x):
    hi = x.astype(BF16)
    lo = (x - hi.astype(F32)).astype(BF16)
    return hi, lo


def _chunk_cumsum(tril_ref, x):
    hi, lo = _split_hi_lo(x)
    lo2 = (x - hi.astype(F32) - lo.astype(F32)).astype(BF16)
    t = tril_ref[...]
    return _dot(t, hi) + _dot(t, lo) + _dot(t, lo2)


def _s5_prep_kernel(lr_ref, li_ref, ldt_ref, lrc_ref, lic_ref, bre_ref, bim_ref, cre_ref, cim_ref,
                    kt_ref, eb_ref, f_ref, pw_ref):
    def discretise(lr, li, dt):
        mag = jnp.exp(lr * dt)
        ab_re = mag * jnp.cos(li * dt)
        ab_im = mag * jnp.sin(li * dt)
        den = lr * lr + li * li
        nr = ab_re - 1.0
        coef_re = (nr * lr + ab_im * li) / den
        coef_im = (ab_im * lr - nr * li) / den
        return ab_re, ab_im, coef_re, coef_im

    lr = lr_ref[...]
    li = li_ref[...]
    dt = jnp.exp(ldt_ref[...])
    ab_re, ab_im, _, _ = discretise(lr, li, dt)
    ones = jnp.ones_like(ab_re)
    zeros = jnp.zeros_like(ab_re)
    low = [(ones, zeros), (ab_re, ab_im)]
    for _ in range(2, S5_R + 1):
        low.append(_cmul(low[-1][0], low[-1][1], ab_re, ab_im))
    l1 = low[S5_R]
    l1_pows = [l1]
    for _ in range(2, S5_PH + 1):
        l1_pows.append(_cmul(l1_pows[-1][0], l1_pows[-1][1], l1[0], l1[1]))
    l2 = l1_pows[S5_PH - 1]
    for r in range(1, S5_PH):
        pw_ref[0, PW_L1 + r - 1] = l1_pows[r - 1][0]
        pw_ref[1, PW_L1 + r - 1] = l1_pows[r - 1][1]
    sq = l2
    for i, k in enumerate(S5_HS_STEPS):
        pw_ref[0, PW_HS + i] = sq[0]
        pw_ref[1, PW_HS + i] = sq[1]
        sq = _cmul(sq[0], sq[1], sq[0], sq[1])
    cur = l2
    for j in range(S5_J):
        pw_ref[0, PW_CARRY + j] = cur[0]
        pw_ref[1, PW_CARRY + j] = cur[1]
        cur = _cmul(cur[0], cur[1], l2[0], l2[1])

    for g in range(S5_GPS):
        c_re = cre_ref[g]
        c_im = cim_ref[g]
        lrc = lrc_ref[g]
        lic = lic_ref[g]
        dtc = jnp.exp(ldt_ref[g:g + 1, :])
        abc_re, abc_im, coef_re, coef_im = discretise(lrc, lic, dtc)
        b_re = bre_ref[g]
        b_im = bim_ref[g]
        bb_re = coef_re * b_re - coef_im * b_im
        bb_im = coef_re * b_im + coef_im * b_re
        col = [(jnp.ones_like(abc_re), jnp.zeros_like(abc_re)), (abc_re, abc_im)]
        for _ in range(2, S5_R):
            col.append(_cmul(col[-1][0], col[-1][1], abc_re, abc_im))
        for t in range(S5_R):
            pr, pi = col[S5_R - 1 - t]
            e_re, e_im = _cmul(pr, pi, bb_re, bb_im)
            eb_ref[g, t, 0] = e_re
            eb_ref[g, t, 1] = e_im
        for k in range(S5_R + 1):
            pr = low[k][0][g:g + 1, :]
            pi = low[k][1][g:g + 1, :]
            cl_re, cl_im = _cmul(c_re, c_im, pr, pi)
            if k >= 1:
                f_ref[g, k - 1, 0] = cl_re
                f_ref[g, k - 1, 1] = -cl_im
            if k < S5_R:
                kt_ref[g, k] = (jnp.dot(cl_re, bb_re, preferred_element_type=F32, precision=lax.Precision.HIGHEST)
                                - jnp.dot(cl_im, bb_im, preferred_element_type=F32, precision=lax.Precision.HIGHEST))


def _s5_prepare(lam_re, lam_im, log_dt, b_re, b_im, c_re, c_im):
    nl = lam_re.shape[0]
    grid = (nl, S5_SLABS)
    g8 = S5_GPS

    def im4(l, j):
        return (l, j, 0, 0)

    def im3(l, j):
        return (l, j, 0)

    kt, eb, f, pw = pl.pallas_call(
        _s5_prep_kernel,
        grid=grid,
        in_specs=[
            pl.BlockSpec((None, g8, S5_STATE), im3),
            pl.BlockSpec((None, g8, S5_STATE), im3),
            pl.BlockSpec((None, g8, 1), im3),
            pl.BlockSpec((None, g8, S5_STATE, 1), im4),
            pl.BlockSpec((None, g8, S5_STATE, 1), im4),
            pl.BlockSpec((None, g8, S5_STATE, S5_GROUP), im4),
            pl.BlockSpec((None, g8, S5_STATE, S5_GROUP), im4),
            pl.BlockSpec((None, g8, S5_GROUP, S5_STATE), im4),
            pl.BlockSpec((None, g8, S5_GROUP, S5_STATE), im4),
        ],
        out_specs=[
            pl.BlockSpec((None, g8, S5_R, S5_GROUP, S5_GROUP), lambda l, j: (l, j, 0, 0, 0)),
            pl.BlockSpec((None, g8, S5_R, 2, S5_STATE, S5_GROUP), lambda l, j: (l, j, 0, 0, 0, 0)),
            pl.BlockSpec((None, g8, S5_R, 2, S5_GROUP, S5_STATE), lambda l, j: (l, j, 0, 0, 0, 0)),
            pl.BlockSpec((None, None, 2, PW_ROWS, g8, S5_STATE), lambda l, j: (l, j, 0, 0, 0, 0)),
        ],
        out_shape=[
            jax.ShapeDtypeStruct((nl, S5_GROUPS, S5_R, S5_GROUP, S5_GROUP), F32),
            jax.ShapeDtypeStruct((nl, S5_GROUPS, S5_R, 2, S5_STATE, S5_GROUP), F32),
            jax.ShapeDtypeStruct((nl, S5_GROUPS, S5_R, 2, S5_GROUP, S5_STATE), F32),
            jax.ShapeDtypeStruct((nl, S5_SLABS, 2, PW_ROWS, g8, S5_STATE), F32),
        ],
        name="s5_prepare",
    )(lam_re, lam_im, log_dt[..., None], lam_re[..., None], lam_im[..., None], b_re, b_im, c_re, c_im)

    eye = jnp.eye(g8, dtype=F32)
    r = S5_R
    kt = kt.reshape(nl, S5_SLABS, g8, r, S5_GROUP, S5_GROUP)
    lag = jnp.arange(r)[None, :] - jnp.arange(r)[:, None]
    ktoe = jnp.where((lag >= 0)[:, :, None, None], kt[:, :, :, jnp.clip(lag, 0)], 0.0)
    toep = jnp.einsum("ljgtrpq,gh->ljtgqrhp", ktoe, eye).reshape(nl, S5_SLABS, r * LANES, r * LANES)
    eb = eb.reshape(nl, S5_SLABS, g8, r, 2, S5_STATE, S5_GROUP)
    ebm = jnp.einsum("ljgtcnq,gh->ljtgqchn", eb, eye).reshape(nl, S5_SLABS, r * LANES, 2 * S5_SW)
    f = f.reshape(nl, S5_SLABS, g8, r, 2, S5_GROUP, S5_STATE)
    fm = jnp.einsum("ljgrcpn,gh->ljcgnrhp", f, eye).reshape(nl, S5_SLABS, 2 * S5_SW, r * LANES)
    pw = pw.reshape(nl, S5_SLABS, 2, PW_ROWS, S5_SW)
    return toep.astype(BF16), ebm.astype(BF16), fm.astype(BF16), pw


def _rmsnorm(x, w):
    return x * lax.rsqrt(jnp.mean(x * x, axis=-1, keepdims=True) + EPS) * w


def _causal_conv(cs_ref, xin, w_ref, b_ref):
    cs_ref[SUBLANES:SUBLANES + TILE, :] = xin
    w = w_ref[...]
    acc = b_ref[...] + w[CONV_W - 1:CONV_W, :] * xin
    for k in range(CONV_W - 1):
        off = SUBLANES - (CONV_W - 1) + k
        acc = acc + w[k:k + 1, :] * cs_ref[pl.ds(off, TILE), :]
    cs_ref[0:SUBLANES, :] = cs_ref[TILE:TILE + SUBLANES, :]
    return acc


def _row_iota(shape):
    return lax.broadcasted_iota(jnp.int32, shape, 0)


def _lane_iota(shape):
    return lax.broadcasted_iota(jnp.int32, shape, 1)


def _lru_scan(a_s, b_s, p2_s, l2_s, c2_s, carry_ref):
    n1 = TILE // 8
    n2 = n1 // 8
    assert n2 == SUBLANES
    row = _row_iota((SUBLANES, LANES))
    for k in range(D_BRANCH // LANES):
        def ph1(ref, r):
            return ref.at[k, pl.ds(r, n1, stride=8), :]

        def ph2(ref, r):
            return ref.at[k, pl.ds(r, n2, stride=8), :]

        p = ph1(a_s, 0)[...]
        l = ph1(b_s, 0)[...]
        for r in range(1, 8):
            ar = ph1(a_s, r)[...]
            br = ph1(b_s, r)[...]
            l = ar * l + br
            p = ar * p
            ph1(a_s, r)[...] = p
            ph1(b_s, r)[...] = l
        p2_s[k] = p
        l2_s[k] = l
        p2 = [ph2(p2_s, 0)[...]]
        l2 = [ph2(l2_s, 0)[...]]
        for r in range(1, 8):
            ar = ph2(p2_s, r)[...]
            br = ph2(l2_s, r)[...]
            l2.append(ar * l2[-1] + br)
            p2.append(ar * p2[-1])
        p3, l3 = p2[7], l2[7]
        for sh in (1, 2, 4):
            psh = jnp.where(row >= sh, pltpu.roll(p3, sh, 0), 1.0)
            lsh = jnp.where(row >= sh, pltpu.roll(l3, sh, 0), 0.0)
            l3 = p3 * lsh + l3
            p3 = p3 * psh
        cin = carry_ref[:, k * LANES:(k + 1) * LANES]
        c3 = l3 + p3 * cin
        c3prev = jnp.where(row >= 1, pltpu.roll(c3, 1, 0), cin)
        carry_ref[:, k * LANES:(k + 1) * LANES] = jnp.broadcast_to(c3[SUBLANES - 1:SUBLANES, :], (SUBLANES, LANES))
        ph2(c2_s, 0)[...] = c3prev
        for r in range(1, 8):
            ph2(c2_s, r)[...] = l2[r - 1] + p2[r - 1] * c3prev
        cprev = c2_s[k]
        for r in range(8):
            ph1(b_s, r)[...] = ph1(b_s, r)[...] + ph1(a_s, r)[...] * cprev


def _mixer_lru(pa, ycat_ref, cs_ref, a_s, b_s, p2_s, l2_s, c2_s, carry_ref,
               cw_ref, cb_ref, wg_ref, bg_ref, ll_ref):
    xa = pa[:, 0:D_BRANCH]
    za = pa[:, D_BRANCH:2 * D_BRANCH]
    u = _causal_conv(cs_ref, xa, cw_ref, cb_ref)
    ub = _bf(u)
    half = D_BRANCH // 2
    rs, is_ = [], []
    for hf in range(2):
        lg = _dot(ub[:, hf * half:(hf + 1) * half], wg_ref[hf]) + bg_ref[:, hf * D_BRANCH:(hf + 1) * D_BRANCH]
        rs.append(_sigmoid(lg[:, 0:half]))
        is_.append(_sigmoid(lg[:, half:2 * half]))
    r = jnp.concatenate(rs, axis=1)
    i = jnp.concatenate(is_, axis=1)
    coef = -LRU_C * _softplus(-ll_ref[...])
    a = jnp.exp(coef * r)
    b = jnp.sqrt(1.0 - a * a) * i * u
    for k in range(D_BRANCH // LANES):
        a_s[k] = a[:, k * LANES:(k + 1) * LANES]
        b_s[k] = b[:, k * LANES:(k + 1) * LANES]
    _lru_scan(a_s, b_s, p2_s, l2_s, c2_s, carry_ref)
    for k in range(D_BRANCH // LANES):
        sl = slice(k * LANES, (k + 1) * LANES)
        ycat_ref[:, sl] = _bf(b_s[k] * _silu(za[:, sl]))


def _shift_rows(c, k, row):
    n = c.shape[0]
    if k % SUBLANES == 0:
        return jnp.concatenate([jnp.zeros((k, c.shape[1]), c.dtype), c[:n - k]], axis=0)
    return jnp.where(row >= k, pltpu.roll(c, k, 0), 0.0)


def _mixer_s5(pb, ycat_ref, u_s, y_s, carry_ref, toep_ref, eb_ref, f_ref, pw_ref, d_ref, gw_ref, gb_ref):
    u = pb[:, 0:D_BRANCH]
    zb = pb[:, D_BRANCH:2 * D_BRANCH]
    for j in range(S5_SLABS):
        u_s[j] = u[:, j * LANES:(j + 1) * LANES]
    J = S5_J
    row = _row_iota((J, S5_SW))
    span = S5_PH * S5_R
    for j in range(S5_SLABS):
        lhs = jnp.concatenate(
            [jnp.concatenate([u_s[j, pl.ds(S5_R * ph + t, J, stride=span), :] for t in range(S5_R)], axis=1)
             for ph in range(S5_PH)], axis=0)
        lhs = _bf(lhs)
        y_loc = _dot(lhs, toep_ref[j])
        e = _dot(lhs, eb_ref[j])

        def pw(i):
            return pw_ref[j, 0, i:i + 1, :], pw_ref[j, 1, i:i + 1, :]

        l1r, l1i = pw(PW_L1)
        lr_ = [e[0:J, 0:S5_SW]]
        li_ = [e[0:J, S5_SW:2 * S5_SW]]
        for ph in range(1, S5_PH):
            mr, mi = _cmul(lr_[-1], li_[-1], l1r, l1i)
            lr_.append(mr + e[ph * J:(ph + 1) * J, 0:S5_SW])
            li_.append(mi + e[ph * J:(ph + 1) * J, S5_SW:2 * S5_SW])
        cr, ci = lr_[-1], li_[-1]
        for idx, k in enumerate(S5_HS_STEPS):
            qr, qi = pw(PW_HS + idx)
            mr, mi = _cmul(_shift_rows(cr, k, row), _shift_rows(ci, k, row), qr, qi)
            cr, ci = cr + mr, ci + mi
        cin_r = carry_ref[j, 0:1, :]
        cin_i = carry_ref[j, 1:2, :]
        tr = pw_ref[j, 0, PW_CARRY:PW_CARRY + J, :]
        ti = pw_ref[j, 1, PW_CARRY:PW_CARRY + J, :]
        mr, mi = _cmul(tr, ti, cin_r, cin_i)
        cr, ci = cr + mr, ci + mi
        carry_ref[j, 0:1, :] = cr[J - 1:J, :]
        carry_ref[j, 1:2, :] = ci[J - 1:J, :]
        cpr = jnp.where(row >= 1, pltpu.roll(cr, 1, 0), cin_r)
        cpi = jnp.where(row >= 1, pltpu.roll(ci, 1, 0), cin_i)
        sp_r, sp_i = [cpr], [cpi]
        for ph in range(1, S5_PH):
            qr, qi = pw(PW_L1 + ph - 1)
            mr, mi = _cmul(cpr, cpi, qr, qi)
            sp_r.append(lr_[ph - 1] + mr)
            sp_i.append(li_[ph - 1] + mi)
        sprev = jnp.concatenate([jnp.concatenate(sp_r, axis=0), jnp.concatenate(sp_i, axis=0)], axis=1)
        y = y_loc + _dot(_bf(sprev), f_ref[j])
        for ph in range(S5_PH):
            for t in range(S5_R):
                y_s[j, pl.ds(S5_R * ph + t, J, stride=span), :] = y[ph * J:(ph + 1) * J, t * LANES:(t + 1) * LANES]
    yb = jnp.concatenate([y_s[j] for j in range(S5_SLABS)], axis=1) + d_ref[...] * u
    yb = _gelu_tanh(yb)
    yb = yb * _sigmoid(_dot(_bf(yb), gw_ref[...]) + gb_ref[...])
    ycat_ref[:, D_BRANCH:2 * D_BRANCH] = _bf(yb * _silu(zb))


def _mixer_gla(pc, ycat_ref, qd_s, ki_s, ke_s, gl_s, v_s, o_s, st_s, tril_ref, wg_ref, bg_ref, nw_ref):
    q = pc[:, 0:GLA_DK]
    k = pc[:, GLA_DK:2 * GLA_DK]
    v = pc[:, 2 * GLA_DK:2 * GLA_DK + D_BRANCH]
    z = pc[:, 2 * GLA_DK + D_BRANCH:2 * GLA_DK + 2 * D_BRANCH]
    glow = pc[:, 2 * GLA_DK + 2 * D_BRANCH:SEG_C_W]
    logits = _dot(_bf(glow), wg_ref[...]) + bg_ref[...]
    g = -_softplus(-logits) / GLA_TAU
    gc = _chunk_cumsum(tril_ref, g)
    qd_s[...] = _bf(q * (GLA_HK ** -0.5) * jnp.exp(gc))
    ki_s[...] = _bf(k * jnp.exp(-gc))
    gl = jnp.concatenate(
        [jnp.broadcast_to(gc[c * CHUNK + CHUNK - 1:c * CHUNK + CHUNK, :], (CHUNK, GLA_DK)) for c in range(NCHUNK)], axis=0)
    ke_s[...] = _bf(k * jnp.exp(gl - gc))
    gl_s[...] = jnp.exp(gl)
    v_s[...] = _bf(v)

    lane = _lane_iota((1, LANES))
    m_lo = (lane < GLA_HK).astype(BF16)
    m_hi = (lane >= GLA_HK).astype(BF16)
    tri = (_lane_iota((CHUNK, CHUNK)) <= _row_iota((CHUNK, CHUNK)))

    def body(c, carry):
        r0 = pl.multiple_of(c * CHUNK, CHUNK)
        rows = pl.ds(r0, CHUNK)
        for tl in range(GLA_DK // LANES):
            cols = slice(tl * LANES, (tl + 1) * LANES)
            qd = qd_s[rows, cols]
            ki = ki_s[rows, cols]
            ke = ke_s[rows, cols]
            dec = gl_s[pl.ds(r0, 1), cols]
            qstack = jnp.concatenate([qd * m_lo, qd * m_hi], axis=0)
            s2 = _dot_nt(qstack, ki)
            for hh in range(2):
                h = 2 * tl + hh
                vh = v_s[rows, h * GLA_HV:(h + 1) * GLA_HV]
                sc = jnp.where(tri, s2[hh * CHUNK:(hh + 1) * CHUNK, :], 0.0)
                qh = qstack[hh * CHUNK:(hh + 1) * CHUNK, :]
                st = st_s[h]
                o = _dot(_bf(sc), vh) + _dot_nt(qh, _bf(st))
                o_s[rows, h * GLA_HV:(h + 1) * GLA_HV] = o
                st_s[h] = st * dec + _dot_tn(vh, ke)
        return carry

    lax.fori_loop(0, NCHUNK, body, 0)
    for h in range(GLA_HEADS):
        sl = slice(h * GLA_HV, (h + 1) * GLA_HV)
        o = o_s[:, sl]
        o = o * lax.rsqrt(jnp.mean(o * o, axis=-1, keepdims=True) + EPS) * nw_ref[...]
        ycat_ref[:, 2 * D_BRANCH + h * GLA_HV:2 * D_BRANCH + (h + 1) * GLA_HV] = _bf(o * _silu(z[:, sl]))


def _mixer_ssd(pd, ycat_ref, cs_ref, xbc_s, ac_s, dt_s, y_s, st_s, tril_ref,
               cw_ref, cb_ref, dtb_ref, alog_ref, dsk_ref, nw_ref):
    z = pd[:, 0:D_BRANCH]
    xbc = _causal_conv(cs_ref, pd[:, D_BRANCH:D_BRANCH + 2 * D_BRANCH], cw_ref, cb_ref)
    xbc_s[...] = _silu(xbc)
    dt = _softplus(pd[:, 3 * D_BRANCH:SEG_D_W] + dtb_ref[...])
    a = -jnp.exp(alog_ref[...])
    dt_s[...] = dt
    ac_s[...] = _chunk_cumsum(tril_ref, dt * a)

    lane = _lane_iota((1, LANES))
    lo = lane < SSD_HDIM
    m_lo = lo.astype(F32)
    m_hi = 1.0 - m_lo
    lane_j = jnp.where(lo, lane, lane - SSD_HDIM)
    tri2 = lane_j <= _row_iota((CHUNK, LANES))
    npairs = SSD_HEADS // 2

    def body(c, carry):
        r0 = pl.multiple_of(c * CHUNK, CHUNK)
        rows = pl.ds(r0, CHUNK)
        ac = ac_s[rows, :]
        dtc = dt_s[rows, :]
        a_last = ac_s[pl.ds(r0 + CHUNK - 1, 1), :]
        dend = jnp.exp(a_last - ac) * dtc
        cdec = jnp.exp(a_last)
        act = jnp.concatenate([ac, ac], axis=0).T
        dtt = jnp.concatenate([dtc, dtc], axis=0).T
        for g in range(SSD_GROUPS):
            bm = xbc_s[rows, D_BRANCH + g * SSD_STATE:D_BRANCH + (g + 1) * SSD_STATE]
            cm = xbc_s[rows, D_BRANCH + (SSD_GROUPS + g) * SSD_STATE:D_BRANCH + (SSD_GROUPS + g + 1) * SSD_STATE]
            bmb = _bf(bm)
            cb2 = _dot_nt(_bf(cm), jnp.concatenate([bmb, bmb], axis=0))
            for pp in range(npairs // SSD_GROUPS):
                pr = g * (npairs // SSD_GROUPS) + pp
                h0, h1 = 2 * pr, 2 * pr + 1
                xp = xbc_s[rows, pr * LANES:(pr + 1) * LANES]
                acol = jnp.where(lo, jnp.broadcast_to(ac[:, h0:h0 + 1], (CHUNK, LANES)),
                                 jnp.broadcast_to(ac[:, h1:h1 + 1], (CHUNK, LANES)))
                arow = jnp.where(lo, act[h0:h0 + 1, :], act[h1:h1 + 1, :])
                dtrow = jnp.where(lo, dtt[h0:h0 + 1, :], dtt[h1:h1 + 1, :])
                seg = jnp.where(tri2, jnp.exp(jnp.where(tri2, acol - arow, 0.0)), 0.0)
                m = cb2 * seg * dtrow
                xbd = jnp.concatenate([xp * m_lo, xp * m_hi], axis=0)
                y = _dot(_bf(m), _bf(xbd))
                eac = jnp.exp(acol)
                cs = jnp.concatenate([cm * jnp.broadcast_to(eac[:, 0:1], (CHUNK, LANES)),
                                      cm * jnp.broadcast_to(eac[:, LANES - 1:LANES], (CHUNK, LANES))], axis=1)
                st = st_s[pr]
                y = y + _dot(_bf(cs), _bf(st))
                y = y + dsk_ref[:, pr * LANES:(pr + 1) * LANES] * xp
                y_s[rows, pr * LANES:(pr + 1) * LANES] = y
                dcol = jnp.where(lo, jnp.broadcast_to(dend[:, h0:h0 + 1], (CHUNK, LANES)),
                                 jnp.broadcast_to(dend[:, h1:h1 + 1], (CHUNK, LANES)))
                snew = _dot_tn(bmb, _bf(xp * dcol))
                cdrow = jnp.where(lo, jnp.broadcast_to(cdec[:, h0:h0 + 1], (1, LANES)),
                                  jnp.broadcast_to(cdec[:, h1:h1 + 1], (1, LANES)))
                st_s[pr] = st * cdrow + jnp.concatenate([snew * m_lo, snew * m_hi], axis=0)
        return carry

    lax.fori_loop(0, NCHUNK, body, 0)
    y = y_s[...] * _silu(z)
    y = y * lax.rsqrt(jnp.mean(y * y, axis=-1, keepdims=True) + EPS) * nw_ref[...]
    ycat_ref[:, 3 * D_BRANCH:4 * D_BRANCH] = _bf(y)


def _layer_kernel(final,
                  x_ref, nw_ref, win_ref,
                  a_cw, a_cb, a_wg, a_bg, a_l,
                  b_toep, b_eb, b_f, b_pw, b_d, b_gw, b_gb,
                  c_wg, c_bg, c_nw,
                  d_cw, d_cb, d_dtb, d_alog, d_dsk, d_nw,
                  wout_ref, nf_ref, tril_ref,
                  o_ref,
                  ycat, cs_a, cs_d, buf0, buf1, p2_s, l2_s, c2_s, lru_carry, s5_carry,
                  qd_s, ki_s, ke_s, gl_s, v_s, o_s, gla_st, xbc_s, ac_s, dt_s, ssd_st):
    @pl.when(pl.program_id(1) == 0)
    def _():
        cs_a[0:SUBLANES, :] = jnp.zeros((SUBLANES, D_BRANCH), F32)
        cs_d[0:SUBLANES, :] = jnp.zeros((SUBLANES, 2 * D_BRANCH), F32)
        lru_carry[...] = jnp.zeros_like(lru_carry)
        s5_carry[...] = jnp.zeros_like(s5_carry)
        gla_st[...] = jnp.zeros_like(gla_st)
        ssd_st[...] = jnp.zeros_like(ssd_st)

    x = x_ref[...]
    hb = _bf(_rmsnorm(x, nw_ref[...]))

    pa = _dot(hb, win_ref[:, SEG_A:SEG_B])
    _mixer_lru(pa, ycat, cs_a, buf0, buf1, p2_s, l2_s, c2_s, lru_carry, a_cw, a_cb, a_wg, a_bg, a_l)
    pb = _dot(hb, win_ref[:, SEG_B:SEG_C])
    _mixer_s5(pb, ycat, buf0, buf1, s5_carry, b_toep, b_eb, b_f, b_pw, b_d, b_gw, b_gb)
    pc = _dot(hb, win_ref[:, SEG_C:SEG_D])
    _mixer_gla(pc, ycat, qd_s, ki_s, ke_s, gl_s, v_s, o_s, gla_st, tril_ref, c_wg, c_bg, c_nw)
    pd = _dot(hb, win_ref[:, SEG_D:N_IN])
    _mixer_ssd(pd, ycat, cs_d, xbc_s, ac_s, dt_s, o_s, ssd_st, tril_ref, d_cw, d_cb, d_dtb, d_alog, d_dsk, d_nw)

    out = x + _dot(ycat[...], wout_ref[...])
    if final:
        out = _rmsnorm(out, nf_ref[...])
    o_ref[...] = out


def _layer_call(layer, final, x, consts):
    bsz, seq, _ = x.shape
    assert seq % TILE == 0
    grid = (bsz, seq // TILE)

    def wspec(arr):
        nd = arr.ndim - 1
        return pl.BlockSpec((None,) + arr.shape[1:], lambda b, s, _n=nd: (layer,) + (0,) * _n,
                            pipeline_mode=pl.Buffered(1))

    def cspec(arr):
        nd = arr.ndim
        return pl.BlockSpec(arr.shape, lambda b, s, _n=nd: (0,) * _n, pipeline_mode=pl.Buffered(1))

    per_layer = consts["per_layer"]
    shared = consts["shared"]
    x_spec = pl.BlockSpec((None, TILE, D_MODEL), lambda b, s: (b, s, 0))
    scratch = [
        pltpu.VMEM((TILE, 4 * D_BRANCH), BF16),
        pltpu.VMEM((TILE + SUBLANES, D_BRANCH), F32),
        pltpu.VMEM((TILE + SUBLANES, 2 * D_BRANCH), F32),
        pltpu.VMEM((4, TILE, LANES), F32),
        pltpu.VMEM((4, TILE, LANES), F32),
        pltpu.VMEM((4, TILE // 8, LANES), F32),
        pltpu.VMEM((4, TILE // 8, LANES), F32),
        pltpu.VMEM((4, TILE // 8, LANES), F32),
        pltpu.VMEM((SUBLANES, D_BRANCH), F32),
        pltpu.VMEM((S5_SLABS, SUBLANES, S5_SW), F32),
        pltpu.VMEM((TILE, GLA_DK), BF16),
        pltpu.VMEM((TILE, GLA_DK), BF16),
        pltpu.VMEM((TILE, GLA_DK), BF16),
        pltpu.VMEM((TILE, GLA_DK), F32),
        pltpu.VMEM((TILE, D_BRANCH), BF16),
        pltpu.VMEM((TILE, D_BRANCH), F32),
        pltpu.VMEM((GLA_HEADS, GLA_HV, LANES), F32),
        pltpu.VMEM((TILE, 2 * D_BRANCH), F32),
        pltpu.VMEM((TILE, LANES), F32),
        pltpu.VMEM((TILE, LANES), F32),
        pltpu.VMEM((SSD_HEADS // 2, 2 * SSD_STATE, LANES), F32),
    ]
    return pl.pallas_call(
        functools.partial(_layer_kernel, final),
        grid=grid,
        in_specs=[x_spec] + [wspec(a) for a in per_layer] + [cspec(a) for a in shared],
        out_specs=x_spec,
        out_shape=jax.ShapeDtypeStruct(x.shape, F32),
        scratch_shapes=scratch,
        compiler_params=pltpu.CompilerParams(
            dimension_semantics=("arbitrary", "arbitrary"),
            vmem_limit_bytes=VMEM_LIMIT_BYTES),
        name=f"layer{layer}",
    )(x, *per_layer, *shared)


def _prepare(norm_w, w_in, lru_conv_w, lru_conv_b, lru_w_r, lru_b_r, lru_w_i, lru_b_i, lru_l,
             s5_lam_re, s5_lam_im, s5_log_dt, s5_b_re, s5_b_im, s5_c_re, s5_c_im, s5_d, s5_glu_w, s5_glu_b,
             gla_w_gate, gla_b_gate, gla_norm_w,
             ssd_conv_w, ssd_conv_b, ssd_dt_bias, ssd_a_log, ssd_d, ssd_norm_w,
             w_out, norm_f_w):
    nl = w_in.shape[0]
    row = lambda a: a.reshape(nl, 1, -1).astype(F32)
    zc = lambda n: jnp.zeros((nl, D_MODEL, n), w_in.dtype)
    w_in_p = jnp.concatenate(
        [w_in[:, :, :ORIG_C_END], zc(SEG_D - ORIG_C_END), w_in[:, :, ORIG_C_END:], zc(N_IN - SEG_D - (ORIG_D_IN - ORIG_C_END))],
        axis=-1).astype(BF16)
    hph = LRU_HEADS // 2
    eye = jnp.eye(hph, dtype=F32)

    def bd(w):
        w = w.reshape(nl, 2, hph, LRU_HDIM, LRU_HDIM)
        return jnp.einsum("lfhij,hk->lfhikj", w, eye).reshape(nl, 2, hph * LRU_HDIM, hph * LRU_HDIM)

    a_wg = jnp.concatenate([bd(lru_w_r), bd(lru_w_i)], axis=-1).astype(BF16)
    half = D_BRANCH // 2
    br = lru_b_r.reshape(nl, 2, half)
    bi = lru_b_i.reshape(nl, 2, half)
    a_bg = jnp.concatenate([br, bi], axis=-1).reshape(nl, 1, 2 * D_BRANCH)
    toep, ebm, fm, pw = _s5_prepare(s5_lam_re, s5_lam_im, s5_log_dt, s5_b_re, s5_b_im, s5_c_re, s5_c_im)
    c_wg = jnp.concatenate([gla_w_gate, jnp.zeros((nl, LANES - GLA_RANK, GLA_DK), gla_w_gate.dtype)], axis=1).astype(BF16)
    pad_h = lambda a: jnp.concatenate([a, jnp.zeros((nl, LANES - SSD_HEADS), a.dtype)], axis=-1).reshape(nl, 1, LANES)
    per_layer = [
        row(norm_w), w_in_p,
        lru_conv_w.astype(F32), row(lru_conv_b), a_wg, a_bg, row(lru_l),
        toep, ebm, fm, pw, row(s5_d), s5_glu_w.astype(BF16), row(s5_glu_b),
        c_wg, row(gla_b_gate), row(gla_norm_w),
        ssd_conv_w.astype(F32), row(ssd_conv_b), pad_h(ssd_dt_bias), pad_h(ssd_a_log),
        row(jnp.repeat(ssd_d, SSD_HDIM, axis=-1)), row(ssd_norm_w),
        w_out.astype(BF16),
    ]
    ti = jnp.arange(TILE)
    tril = ((ti[:, None] // CHUNK == ti[None, :] // CHUNK) & (ti[None, :] <= ti[:, None])).astype(BF16)
    shared = [norm_f_w.reshape(1, D_MODEL).astype(F32), tril]
    return {"per_layer": per_layer, "shared": shared}


def kernel(x, norm_w, w_in, lru_conv_w, lru_conv_b, lru_w_r, lru_b_r, lru_w_i, lru_b_i, lru_l, s5_lam_re, s5_lam_im, s5_log_dt, s5_b_re, s5_b_im, s5_c_re, s5_c_im, s5_d, s5_glu_w, s5_glu_b, gla_w_gate, gla_b_gate, gla_norm_w, ssd_conv_w, ssd_conv_b, ssd_dt_bias, ssd_a_log, ssd_d, ssd_norm_w, w_out, norm_f_w):
    consts = _prepare(norm_w, w_in, lru_conv_w, lru_conv_b, lru_w_r, lru_b_r, lru_w_i, lru_b_i, lru_l,
                      s5_lam_re, s5_lam_im, s5_log_dt, s5_b_re, s5_b_im, s5_c_re, s5_c_im, s5_d, s5_glu_w, s5_glu_b,
                      gla_w_gate, gla_b_gate, gla_norm_w,
                      ssd_conv_w, ssd_conv_b, ssd_dt_bias, ssd_a_log, ssd_d, ssd_norm_w,
                      w_out, norm_f_w)
    nl = w_in.shape[0]
    for layer in range(nl):
        x = _layer_call(layer, layer == nl - 1, x, consts)
    return x
```

```python
import functools
import math

import jax
import jax.numpy as jnp
from jax import lax
from jax.experimental import pallas as pl
from jax.experimental.pallas import tpu as pltpu

F32 = jnp.float32
BF16 = jnp.bfloat16

D_MODEL = 1024
D_BRANCH = 512
CONV_W = 4
EPS = 1e-6
LRU_HEADS = 8
LRU_HDIM = 64
LRU_C = 8.0
S5_GROUP = 16
S5_GROUPS = 32
S5_STATE = 64
GLA_HEADS = 4
GLA_DK = 256
GLA_HK = 64
GLA_HV = 128
GLA_RANK = 16
GLA_TAU = 16.0
SSD_HEADS = 8
SSD_HDIM = 64
SSD_GROUPS = 2
SSD_STATE = 128
CHUNK = 64

LANES = 128
SUBLANES = 8
VMEM_LIMIT_BYTES = 60 * 1024 * 1024

TILE = 512
NCHUNK = TILE // CHUNK
S5_R = 4
S5_PH = 8
S5_NB = TILE // S5_R
S5_J = S5_NB // S5_PH
S5_SLABS = D_BRANCH // LANES
S5_GPS = LANES // S5_GROUP
S5_SW = S5_GPS * S5_STATE
S5_HS_STEPS = tuple(k for k in (1, 2, 4, 8, 16, 32) if k < S5_J)
PW_L1 = 0
PW_HS = S5_PH - 1
PW_CARRY = PW_HS + len(S5_HS_STEPS)
PW_ROWS = PW_CARRY + S5_J

SEG_A = 0
SEG_B = 1024
SEG_C = 2048
SEG_C_W = 1664
SEG_D = SEG_C + SEG_C_W
SEG_D_W = 1664
N_IN = SEG_D + SEG_D_W
ORIG_C_END = 3600
ORIG_D_IN = 5144


def _dot(a, b):
    return jnp.dot(a, b, preferred_element_type=F32)


def _dot_nt(a, b):
    return lax.dot_general(a, b, (((1,), (1,)), ((), ())), preferred_element_type=F32)


def _dot_tn(a, b):
    return lax.dot_general(a, b, (((0,), (0,)), ((), ())), preferred_element_type=F32)


def _bf(x):
    return x.astype(BF16)


def _softplus(x):
    return jnp.maximum(x, 0.0) + jnp.log1p(jnp.exp(-jnp.abs(x)))


def _sigmoid(x):
    return 0.5 * jnp.tanh(0.5 * x) + 0.5


def _silu(x):
    return x * _sigmoid(x)


def _gelu_tanh(x):
    c = math.sqrt(2.0 / math.pi)
    return 0.5 * x * (1.0 + jnp.tanh(c * (x + 0.044715 * (x * x * x))))


def _cmul(ar, ai, br, bi):
    return ar * br - ai * bi, ar * bi + ai * br


def _split_hi_lo(x):
    hi = x.astype(BF16)
    lo = (x - hi.astype(F32)).astype(BF16)
    return hi, lo


def _chunk_cumsum(tril_ref, x):
    hi, lo = _split_hi_lo(x)
    lo2 = (x - hi.astype(F32) - lo.astype(F32)).astype(BF16)
    t = tril_ref[...]
    return _dot(t, hi) + _dot(t, lo) + _dot(t, lo2)


def _s5_prep_kernel(lr_ref, li_ref, ldt_ref, lrc_ref, lic_ref, bre_ref, bim_ref, cre_ref, cim_ref,
                    kt_ref, eb_ref, f_ref, pw_ref):
    def discretise(lr, li, dt):
        mag = jnp.exp(lr * dt)
        ab_re = mag * jnp.cos(li * dt)
        ab_im = mag * jnp.sin(li * dt)
        den = lr * lr + li * li
        nr = ab_re - 1.0
        coef_re = (nr * lr + ab_im * li) / den
        coef_im = (ab_im * lr - nr * li) / den
        return ab_re, ab_im, coef_re, coef_im

    lr = lr_ref[...]
    li = li_ref[...]
    dt = jnp.exp(ldt_ref[...])
    ab_re, ab_im, _, _ = discretise(lr, li, dt)
    ones = jnp.ones_like(ab_re)
    zeros = jnp.zeros_like(ab_re)
    low = [(ones, zeros), (ab_re, ab_im)]
    for _ in range(2, S5_R + 1):
        low.append(_cmul(low[-1][0], low[-1][1], ab_re, ab_im))
    l1 = low[S5_R]
    l1_pows = [l1]
    for _ in range(2, S5_PH + 1):
        l1_pows.append(_cmul(l1_pows[-1][0], l1_pows[-1][1], l1[0], l1[1]))
    l2 = l1_pows[S5_PH - 1]
    for r in range(1, S5_PH):
        pw_ref[0, PW_L1 + r - 1] = l1_pows[r - 1][0]
        pw_ref[1, PW_L1 + r - 1] = l1_pows[r - 1][1]
    sq = l2
    for i, k in enumerate(S5_HS_STEPS):
        pw_ref[0, PW_HS + i] = sq[0]
        pw_ref[1, PW_HS + i] = sq[1]
        sq = _cmul(sq[0], sq[1], sq[0], sq[1])
    cur = l2
    for j in range(S5_J):
        pw_ref[0, PW_CARRY + j] = cur[0]
        pw_ref[1, PW_CARRY + j] = cur[1]
        cur = _cmul(cur[0], cur[1], l2[0], l2[1])

    for g in range(S5_GPS):
        c_re = cre_ref[g]
        c_im = cim_ref[g]
        lrc = lrc_ref[g]
        lic = lic_ref[g]
        dtc = jnp.exp(ldt_ref[g:g + 1, :])
        abc_re, abc_im, coef_re, coef_im = discretise(lrc, lic, dtc)
        b_re = bre_ref[g]
        b_im = bim_ref[g]
        bb_re = coef_re * b_re - coef_im * b_im
        bb_im = coef_re * b_im + coef_im * b_re
        col = [(jnp.ones_like(abc_re), jnp.zeros_like(abc_re)), (abc_re, abc_im)]
        for _ in range(2, S5_R):
            col.append(_cmul(col[-1][0], col[-1][1], abc_re, abc_im))
        for t in range(S5_R):
            pr, pi = col[S5_R - 1 - t]
            e_re, e_im = _cmul(pr, pi, bb_re, bb_im)
            eb_ref[g, t, 0] = e_re
            eb_ref[g, t, 1] = e_im
        for k in range(S5_R + 1):
            pr = low[k][0][g:g + 1, :]
            pi = low[k][1][g:g + 1, :]
            cl_re, cl_im = _cmul(c_re, c_im, pr, pi)
            if k >= 1:
                f_ref[g, k - 1, 0] = cl_re
                f_ref[g, k - 1, 1] = -cl_im
            if k < S5_R:
                kt_ref[g, k] = (jnp.dot(cl_re, bb_re, preferred_element_type=F32, precision=lax.Precision.HIGHEST)
                                - jnp.dot(cl_im, bb_im, preferred_element_type=F32, precision=lax.Precision.HIGHEST))


def _s5_prepare(lam_re, lam_im, log_dt, b_re, b_im, c_re, c_im):
    nl = lam_re.shape[0]
    grid = (nl, S5_SLABS)
    g8 = S5_GPS

    def im4(l, j):
        return (l, j, 0, 0)

    def im3(l, j):
        return (l, j, 0)

    kt, eb, f, pw = pl.pallas_call(
        _s5_prep_kernel,
        grid=grid,
        in_specs=[
            pl.BlockSpec((None, g8, S5_STATE), im3),
            pl.BlockSpec((None, g8, S5_STATE), im3),
            pl.BlockSpec((None, g8, 1), im3),
            pl.BlockSpec((None, g8, S5_STATE, 1), im4),
            pl.BlockSpec((None, g8, S5_STATE, 1), im4),
            pl.BlockSpec((None, g8, S5_STATE, S5_GROUP), im4),
            pl.BlockSpec((None, g8, S5_STATE, S5_GROUP), im4),
            pl.BlockSpec((None, g8, S5_GROUP, S5_STATE), im4),
            pl.BlockSpec((None, g8, S5_GROUP, S5_STATE), im4),
        ],
        out_specs=[
            pl.BlockSpec((None, g8, S5_R, S5_GROUP, S5_GROUP), lambda l, j: (l, j, 0, 0, 0)),
            pl.BlockSpec((None, g8, S5_R, 2, S5_STATE, S5_GROUP), lambda l, j: (l, j, 0, 0, 0, 0)),
            pl.BlockSpec((None, g8, S5_R, 2, S5_GROUP, S5_STATE), lambda l, j: (l, j, 0, 0, 0, 0)),
            pl.BlockSpec((None, None, 2, PW_ROWS, g8, S5_STATE), lambda l, j: (l, j, 0, 0, 0, 0)),
        ],
        out_shape=[
            jax.ShapeDtypeStruct((nl, S5_GROUPS, S5_R, S5_GROUP, S5_GROUP), F32),
            jax.ShapeDtypeStruct((nl, S5_GROUPS, S5_R, 2, S5_STATE, S5_GROUP), F32),
            jax.ShapeDtypeStruct((nl, S5_GROUPS, S5_R, 2, S5_GROUP, S5_STATE), F32),
            jax.ShapeDtypeStruct((nl, S5_SLABS, 2, PW_ROWS, g8, S5_STATE), F32),
        ],
        name="s5_prepare",
    )(lam_re, lam_im, log_dt[..., None], lam_re[..., None], lam_im[..., None], b_re, b_im, c_re, c_im)

    eye = jnp.eye(g8, dtype=F32)
    r = S5_R
    kt = kt.reshape(nl, S5_SLABS, g8, r, S5_GROUP, S5_GROUP)
    lag = jnp.arange(r)[None, :] - jnp.arange(r)[:, None]
    ktoe = jnp.where((lag >= 0)[:, :, None, None], kt[:, :, :, jnp.clip(lag, 0)], 0.0)
    toep = jnp.einsum("ljgtrpq,gh->ljtgqrhp", ktoe, eye).reshape(nl, S5_SLABS, r * LANES, r * LANES)
    eb = eb.reshape(nl, S5_SLABS, g8, r, 2, S5_STATE, S5_GROUP)
    ebm = jnp.einsum("ljgtcnq,gh->ljtgqchn", eb, eye).reshape(nl, S5_SLABS, r * LANES, 2 * S5_SW)
    f = f.reshape(nl, S5_SLABS, g8, r, 2, S5_GROUP, S5_STATE)
    fm = jnp.einsum("ljgrcpn,gh->ljcgnrhp", f, eye).reshape(nl, S5_SLABS, 2 * S5_SW, r * LANES)
    pw = pw.reshape(nl, S5_SLABS, 2, PW_ROWS, S5_SW)
    return toep.astype(BF16), ebm.astype(BF16), fm.astype(BF16), pw


def _rmsnorm(x, w):
    return x * lax.rsqrt(jnp.mean(x * x, axis=-1, keepdims=True) + EPS) * w


def _causal_conv(cs_ref, xin, w_ref, b_ref):
    cs_ref[SUBLANES:SUBLANES + TILE, :] = xin
    w = w_ref[...]
    acc = b_ref[...] + w[CONV_W - 1:CONV_W, :] * xin
    for k in range(CONV_W - 1):
        off = SUBLANES - (CONV_W - 1) + k
        acc = acc + w[k:k + 1, :] * cs_ref[pl.ds(off, TILE), :]
    cs_ref[0:SUBLANES, :] = cs_ref[TILE:TILE + SUBLANES, :]
    return acc


def _row_iota(shape):
    return lax.broadcasted_iota(jnp.int32, shape, 0)


def _lane_iota(shape):
    return lax.broadcasted_iota(jnp.int32, shape, 1)


def _lru_scan(a_s, b_s, p2_s, l2_s, c2_s, carry_ref):
    n1 = TILE // 8
    n2 = n1 // 8
    assert n2 == SUBLANES
    row = _row_iota((SUBLANES, LANES))
    for k in range(D_BRANCH // LANES):
        def ph1(ref, r):
            return ref.at[k, pl.ds(r, n1, stride=8), :]

        def ph2(ref, r):
            return ref.at[k, pl.ds(r, n2, stride=8), :]

        p = ph1(a_s, 0)[...]
        l = ph1(b_s, 0)[...]
        for r in range(1, 8):
            ar = ph1(a_s, r)[...]
            br = ph1(b_s, r)[...]
            l = ar * l + br
            p = ar * p
            ph1(a_s, r)[...] = p
            ph1(b_s, r)[...] = l
        p2_s[k] = p
        l2_s[k] = l
        p2 = [ph2(p2_s, 0)[...]]
        l2 = [ph2(l2_s, 0)[...]]
        for r in range(1, 8):
            ar = ph2(p2_s, r)[...]
            br = ph2(l2_s, r)[...]
            l2.append(ar * l2[-1] + br)
            p2.append(ar * p2[-1])
        p3, l3 = p2[7], l2[7]
        for sh in (1, 2, 4):
            psh = jnp.where(row >= sh, pltpu.roll(p3, sh, 0), 1.0)
            lsh = jnp.where(row >= sh, pltpu.roll(l3, sh, 0), 0.0)
            l3 = p3 * lsh + l3
            p3 = p3 * psh
        cin = carry_ref[:, k * LANES:(k + 1) * LANES]
        c3 = l3 + p3 * cin
        c3prev = jnp.where(row >= 1, pltpu.roll(c3, 1, 0), cin)
        carry_ref[:, k * LANES:(k + 1) * LANES] = jnp.broadcast_to(c3[SUBLANES - 1:SUBLANES, :], (SUBLANES, LANES))
        ph2(c2_s, 0)[...] = c3prev
        for r in range(1, 8):
            ph2(c2_s, r)[...] = l2[r - 1] + p2[r - 1] * c3prev
        cprev = c2_s[k]
        for r in range(8):
            ph1(b_s, r)[...] = ph1(b_s, r)[...] + ph1(a_s, r)[...] * cprev


def _mixer_lru(pa, ycat_ref, cs_ref, a_s, b_s, p2_s, l2_s, c2_s, carry_ref,
               cw_ref, cb_ref, wg_ref, bg_ref, ll_ref):
    xa = pa[:, 0:D_BRANCH]
    za = pa[:, D_BRANCH:2 * D_BRANCH]
    u = _causal_conv(cs_ref, xa, cw_ref, cb_ref)
    ub = _bf(u)
    half = D_BRANCH // 2
    rs, is_ = [], []
    for hf in range(2):
        lg = _dot(ub[:, hf * half:(hf + 1) * half], wg_ref[hf]) + bg_ref[:, hf * D_BRANCH:(hf + 1) * D_BRANCH]
        rs.append(_sigmoid(lg[:, 0:half]))
        is_.append(_sigmoid(lg[:, half:2 * half]))
    r = jnp.concatenate(rs, axis=1)
    i = jnp.concatenate(is_, axis=1)
    coef = -LRU_C * _softplus(-ll_ref[...])
    a = jnp.exp(coef * r)
    b = jnp.sqrt(1.0 - a * a) * i * u
    for k in range(D_BRANCH // LANES):
        a_s[k] = a[:, k * LANES:(k + 1) * LANES]
        b_s[k] = b[:, k * LANES:(k + 1) * LANES]
    _lru_scan(a_s, b_s, p2_s, l2_s, c2_s, carry_ref)
    for k in range(D_BRANCH // LANES):
        sl = slice(k * LANES, (k + 1) * LANES)
        ycat_ref[:, sl] = _bf(b_s[k] * _silu(za[:, sl]))


def _shift_rows(c, k, row):
    n = c.shape[0]
    if k % SUBLANES == 0:
        return jnp.concatenate([jnp.zeros((k, c.shape[1]), c.dtype), c[:n - k]], axis=0)
    return jnp.where(row >= k, pltpu.roll(c, k, 0), 0.0)


def _mixer_s5(pb, ycat_ref, u_s, y_s, carry_ref, toep_ref, eb_ref, f_ref, pw_ref, d_ref, gw_ref, gb_ref):
    u = pb[:, 0:D_BRANCH]
    zb = pb[:, D_BRANCH:2 * D_BRANCH]
    for j in range(S5_SLABS):
        u_s[j] = u[:, j * LANES:(j + 1) * LANES]
    J = S5_J
    row = _row_iota((J, S5_SW))
    span = S5_PH * S5_R
    for j in range(S5_SLABS):
        lhs = jnp.concatenate(
            [jnp.concatenate([u_s[j, pl.ds(S5_R * ph + t, J, stride=span), :] for t in range(S5_R)], axis=1)
             for ph in range(S5_PH)], axis=0)
        lhs = _bf(lhs)
        y_loc = _dot(lhs, toep_ref[j])
        e = _dot(lhs, eb_ref[j])

        def pw(i):
            return pw_ref[j, 0, i:i + 1, :], pw_ref[j, 1, i:i + 1, :]

        l1r, l1i = pw(PW_L1)
        lr_ = [e[0:J, 0:S5_SW]]
        li_ = [e[0:J, S5_SW:2 * S5_SW]]
        for ph in range(1, S5_PH):
            mr, mi = _cmul(lr_[-1], li_[-1], l1r, l1i)
            lr_.append(mr + e[ph * J:(ph + 1) * J, 0:S5_SW])
            li_.append(mi + e[ph * J:(ph + 1) * J, S5_SW:2 * S5_SW])
        cr, ci = lr_[-1], li_[-1]
        for idx, k in enumerate(S5_HS_STEPS):
            qr, qi = pw(PW_HS + idx)
            mr, mi = _cmul(_shift_rows(cr, k, row), _shift_rows(ci, k, row), qr, qi)
            cr, ci = cr + mr, ci + mi
        cin_r = carry_ref[j, 0:1, :]
        cin_i = carry_ref[j, 1:2, :]
        tr = pw_ref[j, 0, PW_CARRY:PW_CARRY + J, :]
        ti = pw_ref[j, 1, PW_CARRY:PW_CARRY + J, :]
        mr, mi = _cmul(tr, ti, cin_r, cin_i)
        cr, ci = cr + mr, ci + mi
        carry_ref[j, 0:1, :] = cr[J - 1:J, :]
        carry_ref[j, 1:2, :] = ci[J - 1:J, :]
        cpr = jnp.where(row >= 1, pltpu.roll(cr, 1, 0), cin_r)
        cpi = jnp.where(row >= 1, pltpu.roll(ci, 1, 0), cin_i)
        sp_r, sp_i = [cpr], [cpi]
        for ph in range(1, S5_PH):
            qr, qi = pw(PW_L1 + ph - 1)
            mr, mi = _cmul(cpr, cpi, qr, qi)
            sp_r.append(lr_[ph - 1] + mr)
            sp_i.append(li_[ph - 1] + mi)
        sprev = jnp.concatenate([jnp.concatenate(sp_r, axis=0), jnp.concatenate(sp_i, axis=0)], axis=1)
        y = y_loc + _dot(_bf(sprev), f_ref[j])
        for ph in range(S5_PH):
            for t in range(S5_R):
                y_s[j, pl.ds(S5_R * ph + t, J, stride=span), :] = y[ph * J:(ph + 1) * J, t * LANES:(t + 1) * LANES]
    yb = jnp.concatenate([y_s[j] for j in range(S5_SLABS)], axis=1) + d_ref[...] * u
    yb = _gelu_tanh(yb)
    yb = yb * _sigmoid(_dot(_bf(yb), gw_ref[...]) + gb_ref[...])
    ycat_ref[:, D_BRANCH:2 * D_BRANCH] = _bf(yb * _silu(zb))


def _mixer_gla(pc, ycat_ref, qd_s, ki_s, ke_s, gl_s, v_s, o_s, st_s, tril_ref, wg_ref, bg_ref, nw_ref):
    q = pc[:, 0:GLA_DK]
    k = pc[:, GLA_DK:2 * GLA_DK]
    v = pc[:, 2 * GLA_DK:2 * GLA_DK + D_BRANCH]
    z = pc[:, 2 * GLA_DK + D_BRANCH:2 * GLA_DK + 2 * D_BRANCH]
    glow = pc[:, 2 * GLA_DK + 2 * D_BRANCH:SEG_C_W]
    logits = _dot(_bf(glow), wg_ref[...]) + bg_ref[...]
    g = -_softplus(-logits) / GLA_TAU
    gc = _chunk_cumsum(tril_ref, g)
    qd_s[...] = _bf(q * (GLA_HK ** -0.5) * jnp.exp(gc))
    ki_s[...] = _bf(k * jnp.exp(-gc))
    gl = jnp.concatenate(
        [jnp.broadcast_to(gc[c * CHUNK + CHUNK - 1:c * CHUNK + CHUNK, :], (CHUNK, GLA_DK)) for c in range(NCHUNK)], axis=0)
    ke_s[...] = _bf(k * jnp.exp(gl - gc))
    gl_s[...] = jnp.exp(gl)
    v_s[...] = _bf(v)

    lane = _lane_iota((1, LANES))
    m_lo = (lane < GLA_HK).astype(BF16)
    m_hi = (lane >= GLA_HK).astype(BF16)
    tri = (_lane_iota((CHUNK, CHUNK)) <= _row_iota((CHUNK, CHUNK)))

    def body(c, carry):
        r0 = pl.multiple_of(c * CHUNK, CHUNK)
        rows = pl.ds(r0, CHUNK)
        for tl in range(GLA_DK // LANES):
            cols = slice(tl * LANES, (tl + 1) * LANES)
            qd = qd_s[rows, cols]
            ki = ki_s[rows, cols]
            ke = ke_s[rows, cols]
            dec = gl_s[pl.ds(r0, 1), cols]
            qstack = jnp.concatenate([qd * m_lo, qd * m_hi], axis=0)
            s2 = _dot_nt(qstack, ki)
            for hh in range(2):
                h = 2 * tl + hh
                vh = v_s[rows, h * GLA_HV:(h + 1) * GLA_HV]
                sc = jnp.where(tri, s2[hh * CHUNK:(hh + 1) * CHUNK, :], 0.0)
                qh = qstack[hh * CHUNK:(hh + 1) * CHUNK, :]
                st = st_s[h]
                o = _dot(_bf(sc), vh) + _dot_nt(qh, _bf(st))
                o_s[rows, h * GLA_HV:(h + 1) * GLA_HV] = o
                st_s[h] = st * dec + _dot_tn(vh, ke)
        return carry

    lax.fori_loop(0, NCHUNK, body, 0, unroll=True)
    for h in range(GLA_HEADS):
        sl = slice(h * GLA_HV, (h + 1) * GLA_HV)
        o = o_s[:, sl]
        o = o * lax.rsqrt(jnp.mean(o * o, axis=-1, keepdims=True) + EPS) * nw_ref[...]
        ycat_ref[:, 2 * D_BRANCH + h * GLA_HV:2 * D_BRANCH + (h + 1) * GLA_HV] = _bf(o * _silu(z[:, sl]))


def _mixer_ssd(pd, ycat_ref, cs_ref, xbc_s, ac_s, dt_s, y_s, st_s, tril_ref,
               cw_ref, cb_ref, dtb_ref, alog_ref, dsk_ref, nw_ref):
    z = pd[:, 0:D_BRANCH]
    xbc = _causal_conv(cs_ref, pd[:, D_BRANCH:D_BRANCH + 2 * D_BRANCH], cw_ref, cb_ref)
    xbc_s[...] = _silu(xbc)
    dt = _softplus(pd[:, 3 * D_BRANCH:SEG_D_W] + dtb_ref[...])
    a = -jnp.exp(alog_ref[...])
    dt_s[...] = dt
    ac_s[...] = _chunk_cumsum(tril_ref, dt * a)

    lane = _lane_iota((1, LANES))
    lo = lane < SSD_HDIM
    m_lo = lo.astype(F32)
    m_hi = 1.0 - m_lo
    lane_j = jnp.where(lo, lane, lane - SSD_HDIM)
    tri2 = lane_j <= _row_iota((CHUNK, LANES))
    npairs = SSD_HEADS // 2

    def body(c, carry):
        r0 = pl.multiple_of(c * CHUNK, CHUNK)
        rows = pl.ds(r0, CHUNK)
        ac = ac_s[rows, :]
        dtc = dt_s[rows, :]
        a_last = ac_s[pl.ds(r0 + CHUNK - 1, 1), :]
        dend = jnp.exp(a_last - ac) * dtc
        cdec = jnp.exp(a_last)
        act = jnp.concatenate([ac, ac], axis=0).T
        dtt = jnp.concatenate([dtc, dtc], axis=0).T
        for g in range(SSD_GROUPS):
            bm = xbc_s[rows, D_BRANCH + g * SSD_STATE:D_BRANCH + (g + 1) * SSD_STATE]
            cm = xbc_s[rows, D_BRANCH + (SSD_GROUPS + g) * SSD_STATE:D_BRANCH + (SSD_GROUPS + g + 1) * SSD_STATE]
            bmb = _bf(bm)
            cb2 = _dot_nt(_bf(cm), jnp.concatenate([bmb, bmb], axis=0))
            for pp in range(npairs // SSD_GROUPS):
                pr = g * (npairs // SSD_GROUPS) + pp
                h0, h1 = 2 * pr, 2 * pr + 1
                xp = xbc_s[rows, pr * LANES:(pr + 1) * LANES]
                acol = jnp.where(lo, jnp.broadcast_to(ac[:, h0:h0 + 1], (CHUNK, LANES)),
                                 jnp.broadcast_to(ac[:, h1:h1 + 1], (CHUNK, LANES)))
                arow = jnp.where(lo, act[h0:h0 + 1, :], act[h1:h1 + 1, :])
                dtrow = jnp.where(lo, dtt[h0:h0 + 1, :], dtt[h1:h1 + 1, :])
                seg = jnp.where(tri2, jnp.exp(jnp.where(tri2, acol - arow, 0.0)), 0.0)
                m = cb2 * seg * dtrow
                xbd = jnp.concatenate([xp * m_lo, xp * m_hi], axis=0)
                y = _dot(_bf(m), _bf(xbd))
                eac = jnp.exp(acol)
                cs = jnp.concatenate([cm * jnp.broadcast_to(eac[:, 0:1], (CHUNK, LANES)),
                                      cm * jnp.broadcast_to(eac[:, LANES - 1:LANES], (CHUNK, LANES))], axis=1)
                st = st_s[pr]
                y = y + _dot(_bf(cs), _bf(st))
                y = y + dsk_ref[:, pr * LANES:(pr + 1) * LANES] * xp
                y_s[rows, pr * LANES:(pr + 1) * LANES] = y
                dcol = jnp.where(lo, jnp.broadcast_to(dend[:, h0:h0 + 1], (CHUNK, LANES)),
                                 jnp.broadcast_to(dend[:, h1:h1 + 1], (CHUNK, LANES)))
                snew = _dot_tn(bmb, _bf(xp * dcol))
                cdrow = jnp.where(lo, jnp.broadcast_to(cdec[:, h0:h0 + 1], (1, LANES)),
                                  jnp.broadcast_to(cdec[:, h1:h1 + 1], (1, LANES)))
                st_s[pr] = st * cdrow + jnp.concatenate([snew * m_lo, snew * m_hi], axis=0)
        return carry

    lax.fori_loop(0, NCHUNK, body, 0, unroll=True)
    y = y_s[...] * _silu(z)
    y = y * lax.rsqrt(jnp.mean(y * y, axis=-1, keepdims=True) + EPS) * nw_ref[...]
    ycat_ref[:, 3 * D_BRANCH:4 * D_BRANCH] = _bf(y)


def _layer_kernel(final,
                  x_ref, nw_ref, win_ref,
                  a_cw, a_cb, a_wg, a_bg, a_l,
                  b_toep, b_eb, b_f, b_pw, b_d, b_gw, b_gb,
                  c_wg, c_bg, c_nw,
                  d_cw, d_cb, d_dtb, d_alog, d_dsk, d_nw,
                  wout_ref, nf_ref, tril_ref,
                  o_ref,
                  ycat, cs_a, cs_d, buf0, buf1, p2_s, l2_s, c2_s, lru_carry, s5_carry,
                  qd_s, ki_s, ke_s, gl_s, v_s, o_s, gla_st, xbc_s, ac_s, dt_s, ssd_st):
    @pl.when(pl.program_id(1) == 0)
    def _():
        cs_a[0:SUBLANES, :] = jnp.zeros((SUBLANES, D_BRANCH), F32)
        cs_d[0:SUBLANES, :] = jnp.zeros((SUBLANES, 2 * D_BRANCH), F32)
        lru_carry[...] = jnp.zeros_like(lru_carry)
        s5_carry[...] = jnp.zeros_like(s5_carry)
        gla_st[...] = jnp.zeros_like(gla_st)
        ssd_st[...] = jnp.zeros_like(ssd_st)

    x = x_ref[...]
    hb = _bf(_rmsnorm(x, nw_ref[...]))

    pa = _dot(hb, win_ref[:, SEG_A:SEG_B])
    _mixer_lru(pa, ycat, cs_a, buf0, buf1, p2_s, l2_s, c2_s, lru_carry, a_cw, a_cb, a_wg, a_bg, a_l)
    pb = _dot(hb, win_ref[:, SEG_B:SEG_C])
    _mixer_s5(pb, ycat, buf0, buf1, s5_carry, b_toep, b_eb, b_f, b_pw, b_d, b_gw, b_gb)
    pc = _dot(hb, win_ref[:, SEG_C:SEG_D])
    _mixer_gla(pc, ycat, qd_s, ki_s, ke_s, gl_s, v_s, o_s, gla_st, tril_ref, c_wg, c_bg, c_nw)
    pd = _dot(hb, win_ref[:, SEG_D:N_IN])
    _mixer_ssd(pd, ycat, cs_d, xbc_s, ac_s, dt_s, o_s, ssd_st, tril_ref, d_cw, d_cb, d_dtb, d_alog, d_dsk, d_nw)

    out = x + _dot(ycat[...], wout_ref[...])
    if final:
        out = _rmsnorm(out, nf_ref[...])
    o_ref[...] = out


def _layer_call(layer, final, x, consts):
    bsz, seq, _ = x.shape
    assert seq % TILE == 0
    grid = (bsz, seq // TILE)

    def wspec(arr):
        nd = arr.ndim - 1
        return pl.BlockSpec((None,) + arr.shape[1:], lambda b, s, _n=nd: (layer,) + (0,) * _n,
                            pipeline_mode=pl.Buffered(1))

    def cspec(arr):
        nd = arr.ndim
        return pl.BlockSpec(arr.shape, lambda b, s, _n=nd: (0,) * _n, pipeline_mode=pl.Buffered(1))

    per_layer = consts["per_layer"]
    shared = consts["shared"]
    x_spec = pl.BlockSpec((None, TILE, D_MODEL), lambda b, s: (b, s, 0))
    scratch = [
        pltpu.VMEM((TILE, 4 * D_BRANCH), BF16),
        pltpu.VMEM((TILE + SUBLANES, D_BRANCH), F32),
        pltpu.VMEM((TILE + SUBLANES, 2 * D_BRANCH), F32),
        pltpu.VMEM((4, TILE, LANES), F32),
        pltpu.VMEM((4, TILE, LANES), F32),
        pltpu.VMEM((4, TILE // 8, LANES), F32),
        pltpu.VMEM((4, TILE // 8, LANES), F32),
        pltpu.VMEM((4, TILE // 8, LANES), F32),
        pltpu.VMEM((SUBLANES, D_BRANCH), F32),
        pltpu.VMEM((S5_SLABS, SUBLANES, S5_SW), F32),
        pltpu.VMEM((TILE, GLA_DK), BF16),
        pltpu.VMEM((TILE, GLA_DK), BF16),
        pltpu.VMEM((TILE, GLA_DK), BF16),
        pltpu.VMEM((TILE, GLA_DK), F32),
        pltpu.VMEM((TILE, D_BRANCH), BF16),
        pltpu.VMEM((TILE, D_BRANCH), F32),
        pltpu.VMEM((GLA_HEADS, GLA_HV, LANES), F32),
        pltpu.VMEM((TILE, 2 * D_BRANCH), F32),
        pltpu.VMEM((TILE, LANES), F32),
        pltpu.VMEM((TILE, LANES), F32),
        pltpu.VMEM((SSD_HEADS // 2, 2 * SSD_STATE, LANES), F32),
    ]
    return pl.pallas_call(
        functools.partial(_layer_kernel, final),
        grid=grid,
        in_specs=[x_spec] + [wspec(a) for a in per_layer] + [cspec(a) for a in shared],
        out_specs=x_spec,
        out_shape=jax.ShapeDtypeStruct(x.shape, F32),
        scratch_shapes=scratch,
        compiler_params=pltpu.CompilerParams(
            dimension_semantics=("arbitrary", "arbitrary"),
            vmem_limit_bytes=VMEM_LIMIT_BYTES),
        name=f"layer{layer}",
    )(x, *per_layer, *shared)


def _prepare(norm_w, w_in, lru_conv_w, lru_conv_b, lru_w_r, lru_b_r, lru_w_i, lru_b_i, lru_l,
             s5_lam_re, s5_lam_im, s5_log_dt, s5_b_re, s5_b_im, s5_c_re, s5_c_im, s5_d, s5_glu_w, s5_glu_b,
             gla_w_gate, gla_b_gate, gla_norm_w,
             ssd_conv_w, ssd_conv_b, ssd_dt_bias, ssd_a_log, ssd_d, ssd_norm_w,
             w_out, norm_f_w):
    nl = w_in.shape[0]
    row = lambda a: a.reshape(nl, 1, -1).astype(F32)
    zc = lambda n: jnp.zeros((nl, D_MODEL, n), w_in.dtype)
    w_in_p = jnp.concatenate(
        [w_in[:, :, :ORIG_C_END], zc(SEG_D - ORIG_C_END), w_in[:, :, ORIG_C_END:], zc(N_IN - SEG_D - (ORIG_D_IN - ORIG_C_END))],
        axis=-1).astype(BF16)
    hph = LRU_HEADS // 2
    eye = jnp.eye(hph, dtype=F32)

    def bd(w):
        w = w.reshape(nl, 2, hph, LRU_HDIM, LRU_HDIM)
        return jnp.einsum("lfhij,hk->lfhikj", w, eye).reshape(nl, 2, hph * LRU_HDIM, hph * LRU_HDIM)

    a_wg = jnp.concatenate([bd(lru_w_r), bd(lru_w_i)], axis=-1).astype(BF16)
    half = D_BRANCH // 2
    br = lru_b_r.reshape(nl, 2, half)
    bi = lru_b_i.reshape(nl, 2, half)
    a_bg = jnp.concatenate([br, bi], axis=-1).reshape(nl, 1, 2 * D_BRANCH)
    toep, ebm, fm, pw = _s5_prepare(s5_lam_re, s5_lam_im, s5_log_dt, s5_b_re, s5_b_im, s5_c_re, s5_c_im)
    c_wg = jnp.concatenate([gla_w_gate, jnp.zeros((nl, LANES - GLA_RANK, GLA_DK), gla_w_gate.dtype)], axis=1).astype(BF16)
    pad_h = lambda a: jnp.concatenate([a, jnp.zeros((nl, LANES - SSD_HEADS), a.dtype)], axis=-1).reshape(nl, 1, LANES)
    per_layer = [
        row(norm_w), w_in_p,
        lru_conv_w.astype(F32), row(lru_conv_b), a_wg, a_bg, row(lru_l),
        toep, ebm, fm, pw, row(s5_d), s5_glu_w.astype(BF16), row(s5_glu_b),
        c_wg, row(gla_b_gate), row(gla_norm_w),
        ssd_conv_w.astype(F32), row(ssd_conv_b), pad_h(ssd_dt_bias), pad_h(ssd_a_log),
        row(jnp.repeat(ssd_d, SSD_HDIM, axis=-1)), row(ssd_norm_w),
        w_out.astype(BF16),
    ]
    ti = jnp.arange(TILE)
    tril = ((ti[:, None] // CHUNK == ti[None, :] // CHUNK) & (ti[None, :] <= ti[:, None])).astype(BF16)
    shared = [norm_f_w.reshape(1, D_MODEL).astype(F32), tril]
    return {"per_layer": per_layer, "shared": shared}


def kernel(x, norm_w, w_in, lru_conv_w, lru_conv_b, lru_w_r, lru_b_r, lru_w_i, lru_b_i, lru_l, s5_lam_re, s5_lam_im, s5_log_dt, s5_b_re, s5_b_im, s5_c_re, s5_c_im, s5_d, s5_glu_w, s5_glu_b, gla_w_gate, gla_b_gate, gla_norm_w, ssd_conv_w, ssd_conv_b, ssd_dt_bias, ssd_a_log, ssd_d, ssd_norm_w, w_out, norm_f_w):
    consts = _prepare(norm_w, w_in, lru_conv_w, lru_conv_b, lru_w_r, lru_b_r, lru_w_i, lru_b_i, lru_l,
                      s5_lam_re, s5_lam_im, s5_log_dt, s5_b_re, s5_b_im, s5_c_re, s5_c_im, s5_d, s5_glu_w, s5_glu_b,
                      gla_w_gate, gla_b_gate, gla_norm_w,
                      ssd_conv_w, ssd_conv_b, ssd_dt_bias, ssd_a_log, ssd_d, ssd_norm_w,
                      w_out, norm_f_w)
    nl = w_in.shape[0]
    for layer in range(nl):
        x = _layer_call(layer, layer == nl - 1, x, consts)
    return x
```

```python
import functools
import math

import jax
import jax.numpy as jnp
from jax import lax
from jax.experimental import pallas as pl
from jax.experimental.pallas import tpu as pltpu

F32 = jnp.float32
BF16 = jnp.bfloat16

D_MODEL = 1024
D_BRANCH = 512
CONV_W = 4
EPS = 1e-6
LRU_HEADS = 8
LRU_HDIM = 64
LRU_C = 8.0
S5_GROUP = 16
S5_GROUPS = 32
S5_STATE = 64
GLA_HEADS = 4
GLA_DK = 256
GLA_HK = 64
GLA_HV = 128
GLA_RANK = 16
GLA_TAU = 16.0
SSD_HEADS = 8
SSD_HDIM = 64
SSD_GROUPS = 2
SSD_STATE = 128
CHUNK = 64

LANES = 128
SUBLANES = 8
VMEM_LIMIT_BYTES = 62 * 1024 * 1024

TILE = 512
NCHUNK = TILE // CHUNK
S5_R = 4
S5_PH = 8
S5_NB = TILE // S5_R
S5_J = S5_NB // S5_PH
S5_SLABS = D_BRANCH // LANES
S5_GPS = LANES // S5_GROUP
S5_SW = S5_GPS * S5_STATE
S5_HS_STEPS = tuple(k for k in (1, 2, 4, 8, 16, 32) if k < S5_J)
PW_L1 = 0
PW_HS = S5_PH - 1
PW_CARRY = PW_HS + len(S5_HS_STEPS)
PW_ROWS = PW_CARRY + S5_J

SEG_A = 0
SEG_B = 1024
SEG_C = 2048
SEG_C_W = 1664
SEG_D = SEG_C + SEG_C_W
SEG_D_W = 1664
N_IN = SEG_D + SEG_D_W
ORIG_C_END = 3600
ORIG_D_IN = 5144


def _dot(a, b):
    return jnp.dot(a, b, preferred_element_type=F32)


def _dot_nt(a, b):
    return lax.dot_general(a, b, (((1,), (1,)), ((), ())), preferred_element_type=F32)


def _dot_tn(a, b):
    return lax.dot_general(a, b, (((0,), (0,)), ((), ())), preferred_element_type=F32)


def _bf(x):
    return x.astype(BF16)


def _softplus(x):
    return jnp.maximum(x, 0.0) + jnp.log1p(jnp.exp(-jnp.abs(x)))


def _sigmoid(x):
    return 0.5 * jnp.tanh(0.5 * x) + 0.5


def _silu(x):
    return x * _sigmoid(x)


def _gelu_tanh(x):
    c = math.sqrt(2.0 / math.pi)
    return 0.5 * x * (1.0 + jnp.tanh(c * (x + 0.044715 * (x * x * x))))


def _cmul(ar, ai, br, bi):
    return ar * br - ai * bi, ar * bi + ai * br


def _split_hi_lo(x):
    hi = x.astype(BF16)
    lo = (x - hi.astype(F32)).astype(BF16)
    return hi, lo


def _chunk_cumsum(x):
    hi, lo = _split_hi_lo(x)
    lo2 = (x - hi.astype(F32) - lo.astype(F32)).astype(BF16)
    ri = lax.broadcasted_iota(jnp.int32, (CHUNK, CHUNK), 0)
    ci = lax.broadcasted_iota(jnp.int32, (CHUNK, CHUNK), 1)
    t = (ci <= ri).astype(BF16)
    out = []
    for c in range(x.shape[0] // CHUNK):
        rows = slice(c * CHUNK, (c + 1) * CHUNK)
        out.append(_dot(t, hi[rows]) + _dot(t, lo[rows]) + _dot(t, lo2[rows]))
    return jnp.concatenate(out, axis=0)


def _s5_prep_kernel(lr_ref, li_ref, ldt_ref, lrc_ref, lic_ref, bt_re_ref, bt_im_ref, c_re_ref, c_im_ref,
                    ct_re_ref, ct_im_ref, toep_ref, eb_ref, f_ref, pw_ref):
    def discretise(lr, li, dt):
        mag = jnp.exp(lr * dt)
        ab_re = mag * jnp.cos(li * dt)
        ab_im = mag * jnp.sin(li * dt)
        den = lr * lr + li * li
        nr = ab_re - 1.0
        coef_re = (nr * lr + ab_im * li) / den
        coef_im = (ab_im * lr - nr * li) / den
        return ab_re, ab_im, coef_re, coef_im

    def powers(ab_re, ab_im, n):
        out = [(jnp.ones_like(ab_re), jnp.zeros_like(ab_re)), (ab_re, ab_im)]
        for _ in range(2, n + 1):
            out.append(_cmul(out[-1][0], out[-1][1], ab_re, ab_im))
        return out

    ab_re, ab_im, cf_re, cf_im = discretise(lr_ref[...], li_ref[...], jnp.exp(ldt_ref[...]))
    low = powers(ab_re, ab_im, S5_R)
    l1 = low[S5_R]
    l1_pows = [l1]
    for _ in range(2, S5_PH + 1):
        l1_pows.append(_cmul(l1_pows[-1][0], l1_pows[-1][1], l1[0], l1[1]))
    l2 = l1_pows[S5_PH - 1]
    for r in range(1, S5_PH):
        pw_ref[0, PW_L1 + r - 1] = l1_pows[r - 1][0]
        pw_ref[1, PW_L1 + r - 1] = l1_pows[r - 1][1]
    sq = l2
    for i, k in enumerate(S5_HS_STEPS):
        pw_ref[0, PW_HS + i] = sq[0]
        pw_ref[1, PW_HS + i] = sq[1]
        sq = _cmul(sq[0], sq[1], sq[0], sq[1])
    cur = l2
    for j in range(S5_J):
        pw_ref[0, PW_CARRY + j] = cur[0]
        pw_ref[1, PW_CARRY + j] = cur[1]
        cur = _cmul(cur[0], cur[1], l2[0], l2[1])

    toep_ref[...] = jnp.zeros_like(toep_ref)
    eb_ref[...] = jnp.zeros_like(eb_ref)
    f_ref[...] = jnp.zeros_like(f_ref)
    hi = lax.Precision.HIGHEST
    for g in range(S5_GPS):
        row_pw = [(p[0][g:g + 1, :], p[1][g:g + 1, :]) for p in low]
        bbt_re, bbt_im = _cmul(cf_re[g:g + 1, :], cf_im[g:g + 1, :], bt_re_ref[g], bt_im_ref[g])
        for t in range(S5_R):
            e_re, e_im = _cmul(row_pw[S5_R - 1 - t][0], row_pw[S5_R - 1 - t][1], bbt_re, bbt_im)
            rows = slice(t * LANES + g * S5_GROUP, t * LANES + (g + 1) * S5_GROUP)
            eb_ref[rows, g * S5_STATE:(g + 1) * S5_STATE] = _bf(e_re)
            eb_ref[rows, S5_SW + g * S5_STATE:S5_SW + (g + 1) * S5_STATE] = _bf(e_im)
        for k in range(S5_R):
            cl_re, cl_im = _cmul(c_re_ref[g], c_im_ref[g], row_pw[k][0], row_pw[k][1])
            kt = (lax.dot_general(bbt_re, cl_re, (((1,), (1,)), ((), ())), preferred_element_type=F32, precision=hi)
                  - lax.dot_general(bbt_im, cl_im, (((1,), (1,)), ((), ())), preferred_element_type=F32, precision=hi))
            ktb = _bf(kt)
            for t in range(S5_R - k):
                r = t + k
                toep_ref[t * LANES + g * S5_GROUP:t * LANES + (g + 1) * S5_GROUP,
                         r * LANES + g * S5_GROUP:r * LANES + (g + 1) * S5_GROUP] = ktb
        abc_re, abc_im, _, _ = discretise(lrc_ref[g], lic_ref[g], jnp.exp(ldt_ref[g:g + 1, :]))
        col_pw = powers(abc_re, abc_im, S5_R)
        for r in range(S5_R):
            f_re, f_im = _cmul(ct_re_ref[g], ct_im_ref[g], col_pw[r + 1][0], col_pw[r + 1][1])
            cols = slice(r * LANES + g * S5_GROUP, r * LANES + (g + 1) * S5_GROUP)
            f_ref[g * S5_STATE:(g + 1) * S5_STATE, cols] = _bf(f_re)
            f_ref[S5_SW + g * S5_STATE:S5_SW + (g + 1) * S5_STATE, cols] = _bf(-f_im)


def _s5_prepare(lam_re, lam_im, log_dt, b_re, b_im, c_re, c_im):
    nl = lam_re.shape[0]
    g8 = S5_GPS
    rl = S5_R * LANES

    def im4(l, j):
        return (l, j, 0, 0)

    def im3(l, j):
        return (l, j, 0)

    gsn = pl.BlockSpec((None, g8, S5_GROUP, S5_STATE), im4)
    toep, ebm, fm, pw = pl.pallas_call(
        _s5_prep_kernel,
        grid=(nl, S5_SLABS),
        in_specs=[
            pl.BlockSpec((None, g8, S5_STATE), im3),
            pl.BlockSpec((None, g8, S5_STATE), im3),
            pl.BlockSpec((None, g8, 1), im3),
            pl.BlockSpec((None, g8, S5_STATE, 1), im4),
            pl.BlockSpec((None, g8, S5_STATE, 1), im4),
            gsn, gsn, gsn, gsn,
            pl.BlockSpec((None, g8, S5_STATE, S5_GROUP), im4),
            pl.BlockSpec((None, g8, S5_STATE, S5_GROUP), im4),
        ],
        out_specs=[
            pl.BlockSpec((None, None, rl, rl), im4),
            pl.BlockSpec((None, None, rl, 2 * S5_SW), im4),
            pl.BlockSpec((None, None, 2 * S5_SW, rl), im4),
            pl.BlockSpec((None, None, 2, PW_ROWS, g8, S5_STATE), lambda l, j: (l, j, 0, 0, 0, 0)),
        ],
        out_shape=[
            jax.ShapeDtypeStruct((nl, S5_SLABS, rl, rl), BF16),
            jax.ShapeDtypeStruct((nl, S5_SLABS, rl, 2 * S5_SW), BF16),
            jax.ShapeDtypeStruct((nl, S5_SLABS, 2 * S5_SW, rl), BF16),
            jax.ShapeDtypeStruct((nl, S5_SLABS, 2, PW_ROWS, g8, S5_STATE), F32),
        ],
        name="s5_prepare",
    )(lam_re, lam_im, log_dt[..., None], lam_re[..., None], lam_im[..., None],
      jnp.swapaxes(b_re, -1, -2), jnp.swapaxes(b_im, -1, -2), c_re, c_im,
      jnp.swapaxes(c_re, -1, -2), jnp.swapaxes(c_im, -1, -2))
    return toep, ebm, fm, pw.reshape(nl, S5_SLABS, 2, PW_ROWS, S5_SW)


def _rmsnorm(x, w):
    return x * lax.rsqrt(jnp.mean(x * x, axis=-1, keepdims=True) + EPS) * w


def _causal_conv(p_ref, c0, c1, w_ref, b_ref):
    w = w_ref[...]
    acc = b_ref[...]
    for k in range(CONV_W):
        off = SUBLANES - (CONV_W - 1) + k
        acc = acc + w[k:k + 1, :] * p_ref[pl.ds(off, TILE), c0:c1]
    return acc


def _keep_tail(p_ref):
    p_ref[0:SUBLANES, :] = p_ref[TILE:TILE + SUBLANES, :]


def _row_iota(shape):
    return lax.broadcasted_iota(jnp.int32, shape, 0)


def _lane_iota(shape):
    return lax.broadcasted_iota(jnp.int32, shape, 1)


def _lru_scan(a_s, b_s, p2_s, l2_s, c2_s, carry_ref, tick):
    n1 = TILE // 8
    n2 = n1 // 8
    assert n2 == SUBLANES
    row = _row_iota((SUBLANES, LANES))
    for k in range(D_BRANCH // LANES):
        def ph1(ref, r):
            return ref.at[k, pl.ds(r, n1, stride=8), :]

        def ph2(ref, r):
            return ref.at[k, pl.ds(r, n2, stride=8), :]

        p = ph1(a_s, 0)[...]
        l = ph1(b_s, 0)[...]
        for r in range(1, 8):
            ar = ph1(a_s, r)[...]
            br = ph1(b_s, r)[...]
            l = ar * l + br
            p = ar * p
            ph1(a_s, r)[...] = p
            ph1(b_s, r)[...] = l
        p2_s[k] = p
        l2_s[k] = l
        p2 = [ph2(p2_s, 0)[...]]
        l2 = [ph2(l2_s, 0)[...]]
        for r in range(1, 8):
            ar = ph2(p2_s, r)[...]
            br = ph2(l2_s, r)[...]
            l2.append(ar * l2[-1] + br)
            p2.append(ar * p2[-1])
        p3, l3 = p2[7], l2[7]
        for sh in (1, 2, 4):
            psh = jnp.where(row >= sh, pltpu.roll(p3, sh, 0), 1.0)
            lsh = jnp.where(row >= sh, pltpu.roll(l3, sh, 0), 0.0)
            l3 = p3 * lsh + l3
            p3 = p3 * psh
        cin = carry_ref[:, k * LANES:(k + 1) * LANES]
        c3 = l3 + p3 * cin
        c3prev = jnp.where(row >= 1, pltpu.roll(c3, 1, 0), cin)
        carry_ref[:, k * LANES:(k + 1) * LANES] = jnp.broadcast_to(c3[SUBLANES - 1:SUBLANES, :], (SUBLANES, LANES))
        ph2(c2_s, 0)[...] = c3prev
        for r in range(1, 8):
            ph2(c2_s, r)[...] = l2[r - 1] + p2[r - 1] * c3prev
        cprev = c2_s[k]
        for r in range(8):
            ph1(b_s, r)[...] = ph1(b_s, r)[...] + ph1(a_s, r)[...] * cprev
        tick()


def _mixer_lru(pa_ref, ycat_ref, a_s, b_s, p2_s, l2_s, c2_s, carry_ref,
               cw_ref, cb_ref, wg_ref, bg_ref, ll_ref, tick):
    u = _causal_conv(pa_ref, 0, D_BRANCH, cw_ref, cb_ref)
    tick()
    ub = _bf(u)
    half = D_BRANCH // 2
    rs, is_ = [], []
    for hf in range(2):
        lg = _dot(ub[:, hf * half:(hf + 1) * half], wg_ref[hf]) + bg_ref[:, hf * D_BRANCH:(hf + 1) * D_BRANCH]
        rs.append(_sigmoid(lg[:, 0:half]))
        is_.append(_sigmoid(lg[:, half:2 * half]))
    r = jnp.concatenate(rs, axis=1)
    i = jnp.concatenate(is_, axis=1)
    coef = -LRU_C * _softplus(-ll_ref[...])
    a = jnp.exp(coef * r)
    b = jnp.sqrt(1.0 - a * a) * i * u
    for k in range(D_BRANCH // LANES):
        a_s[k] = a[:, k * LANES:(k + 1) * LANES]
        b_s[k] = b[:, k * LANES:(k + 1) * LANES]
    tick()
    _lru_scan(a_s, b_s, p2_s, l2_s, c2_s, carry_ref, tick)
    for k in range(D_BRANCH // LANES):
        ycat_ref[:, k * LANES:(k + 1) * LANES] = _bf(
            b_s[k] * _silu(pa_ref[SUBLANES:SUBLANES + TILE, D_BRANCH + k * LANES:D_BRANCH + (k + 1) * LANES]))
    _keep_tail(pa_ref)


def _shift_rows(c, k, row):
    n = c.shape[0]
    if k % SUBLANES == 0:
        return jnp.concatenate([jnp.zeros((k, c.shape[1]), c.dtype), c[:n - k]], axis=0)
    return jnp.where(row >= k, pltpu.roll(c, k, 0), 0.0)


def _mixer_s5(pb_ref, ycat_ref, u_s, y_s, carry_ref, toep_ref, eb_ref, f_ref, pw_ref, d_ref, gw_ref, gb_ref, tick):
    for j in range(S5_SLABS):
        u_s[j] = pb_ref[:, j * LANES:(j + 1) * LANES]
    J = S5_J
    row = _row_iota((J, S5_SW))
    span = S5_PH * S5_R
    for j in range(S5_SLABS):
        lhs = jnp.concatenate(
            [jnp.concatenate([u_s[j, pl.ds(S5_R * ph + t, J, stride=span), :] for t in range(S5_R)], axis=1)
             for ph in range(S5_PH)], axis=0)
        lhs = _bf(lhs)
        y_loc = _dot(lhs, toep_ref[j])
        e = _dot(lhs, eb_ref[j])

        def pw(i):
            return pw_ref[j, 0, i:i + 1, :], pw_ref[j, 1, i:i + 1, :]

        l1r, l1i = pw(PW_L1)
        lr_ = [e[0:J, 0:S5_SW]]
        li_ = [e[0:J, S5_SW:2 * S5_SW]]
        for ph in range(1, S5_PH):
            mr, mi = _cmul(lr_[-1], li_[-1], l1r, l1i)
            lr_.append(mr + e[ph * J:(ph + 1) * J, 0:S5_SW])
            li_.append(mi + e[ph * J:(ph + 1) * J, S5_SW:2 * S5_SW])
        cr, ci = lr_[-1], li_[-1]
        for idx, k in enumerate(S5_HS_STEPS):
            qr, qi = pw(PW_HS + idx)
            mr, mi = _cmul(_shift_rows(cr, k, row), _shift_rows(ci, k, row), qr, qi)
            cr, ci = cr + mr, ci + mi
        cin_r = carry_ref[j, 0:1, :]
        cin_i = carry_ref[j, 1:2, :]
        tr = pw_ref[j, 0, PW_CARRY:PW_CARRY + J, :]
        ti = pw_ref[j, 1, PW_CARRY:PW_CARRY + J, :]
        mr, mi = _cmul(tr, ti, cin_r, cin_i)
        cr, ci = cr + mr, ci + mi
        carry_ref[j, 0:1, :] = cr[J - 1:J, :]
        carry_ref[j, 1:2, :] = ci[J - 1:J, :]
        cpr = jnp.where(row >= 1, pltpu.roll(cr, 1, 0), cin_r)
        cpi = jnp.where(row >= 1, pltpu.roll(ci, 1, 0), cin_i)
        sp_r, sp_i = [cpr], [cpi]
        for ph in range(1, S5_PH):
            qr, qi = pw(PW_L1 + ph - 1)
            mr, mi = _cmul(cpr, cpi, qr, qi)
            sp_r.append(lr_[ph - 1] + mr)
            sp_i.append(li_[ph - 1] + mi)
        sprev = jnp.concatenate([jnp.concatenate(sp_r, axis=0), jnp.concatenate(sp_i, axis=0)], axis=1)
        y = y_loc + _dot(_bf(sprev), f_ref[j])
        for ph in range(S5_PH):
            for t in range(S5_R):
                y_s[j, pl.ds(S5_R * ph + t, J, stride=span), :] = y[ph * J:(ph + 1) * J, t * LANES:(t + 1) * LANES]
        tick()
    yb = jnp.concatenate([y_s[j] for j in range(S5_SLABS)], axis=1) + d_ref[...] * pb_ref[:, 0:D_BRANCH]
    yb = _gelu_tanh(yb)
    tick()
    yb = yb * _sigmoid(_dot(_bf(yb), gw_ref[...]) + gb_ref[...])
    ycat_ref[:, D_BRANCH:2 * D_BRANCH] = _bf(yb * _silu(pb_ref[:, D_BRANCH:2 * D_BRANCH]))


def _mixer_gla(pc_ref, ycat_ref, qd_s, ki_s, ke_s, gl_s, o_s, st_s, wg_ref, bg_ref, nw_ref, tick):
    k = pc_ref[:, GLA_DK:2 * GLA_DK]
    z_off = 2 * GLA_DK + D_BRANCH
    glow = pc_ref[:, 2 * GLA_DK + 2 * D_BRANCH:SEG_C_W]
    logits = _dot(_bf(glow), wg_ref[...]) + bg_ref[...]
    g = -_softplus(-logits) / GLA_TAU
    gc = _chunk_cumsum(g)
    tick()
    qd_s[...] = _bf(pc_ref[:, 0:GLA_DK] * (GLA_HK ** -0.5) * jnp.exp(gc))
    ki_s[...] = _bf(k * jnp.exp(-gc))
    gl = jnp.concatenate(
        [jnp.broadcast_to(gc[c * CHUNK + CHUNK - 1:c * CHUNK + CHUNK, :], (CHUNK, GLA_DK)) for c in range(NCHUNK)], axis=0)
    ke_s[...] = _bf(k * jnp.exp(gl - gc))
    for c in range(NCHUNK):
        gl_s[c * SUBLANES:(c + 1) * SUBLANES, :] = jnp.exp(gl[c * CHUNK:c * CHUNK + SUBLANES, :])
    tick()

    lane = _lane_iota((1, LANES))
    m_lo = (lane < GLA_HK).astype(BF16)
    m_hi = (lane >= GLA_HK).astype(BF16)
    tri = (_lane_iota((CHUNK, CHUNK)) <= _row_iota((CHUNK, CHUNK)))

    for c in range(NCHUNK):
        r0 = c * CHUNK
        rows = pl.ds(r0, CHUNK)
        for tl in range(GLA_DK // LANES):
            cols = slice(tl * LANES, (tl + 1) * LANES)
            qd = qd_s[rows, cols]
            ki = ki_s[rows, cols]
            ke = ke_s[rows, cols]
            dec = gl_s[pl.ds(c * SUBLANES, 1), cols]
            qstack = jnp.concatenate([qd * m_lo, qd * m_hi], axis=0)
            s2 = _dot_nt(qstack, ki)
            for hh in range(2):
                h = 2 * tl + hh
                vh = _bf(pc_ref[rows, 2 * GLA_DK + h * GLA_HV:2 * GLA_DK + (h + 1) * GLA_HV])
                sc = jnp.where(tri, s2[hh * CHUNK:(hh + 1) * CHUNK, :], 0.0)
                qh = qstack[hh * CHUNK:(hh + 1) * CHUNK, :]
                st = st_s[h]
                o = _dot(_bf(sc), vh) + _dot_nt(qh, _bf(st))
                o_s[h, rows, :] = o
                st_s[h] = st * dec + _dot_tn(vh, ke)
        tick()

    for h in range(GLA_HEADS):
        o = o_s[h]
        o = o * lax.rsqrt(jnp.mean(o * o, axis=-1, keepdims=True) + EPS) * nw_ref[...]
        zh = pc_ref[:, z_off + h * GLA_HV:z_off + (h + 1) * GLA_HV]
        ycat_ref[:, 2 * D_BRANCH + h * GLA_HV:2 * D_BRANCH + (h + 1) * GLA_HV] = _bf(o * _silu(zh))


def _mixer_ssd(pd_ref, ycat_ref, xbc_s, ac_s, dt_s, y_s, st_s,
               cw_ref, cb_ref, dtb_ref, alog_ref, dsk_ref, nw_ref, tick):
    body = pd_ref.at[pl.ds(SUBLANES, TILE)]
    xbc = _causal_conv(pd_ref, D_BRANCH, 3 * D_BRANCH, cw_ref, cb_ref)
    tick()
    xbc_s[...] = _silu(xbc)
    dt = _softplus(body[:, 3 * D_BRANCH:SEG_D_W] + dtb_ref[...])
    a = -jnp.exp(alog_ref[...])
    dt_s[...] = dt
    ac_s[...] = _chunk_cumsum(dt * a)
    tick()

    lane = _lane_iota((1, LANES))
    lo = lane < SSD_HDIM
    m_lo = lo.astype(F32)
    m_hi = 1.0 - m_lo
    lane_j = jnp.where(lo, lane, lane - SSD_HDIM)
    tri2 = lane_j <= _row_iota((CHUNK, LANES))
    npairs = SSD_HEADS // 2

    for c in range(NCHUNK):
        r0 = c * CHUNK
        rows = pl.ds(r0, CHUNK)
        ac = ac_s[rows, :]
        dtc = dt_s[rows, :]
        a_last = ac_s[pl.ds(r0 + CHUNK - 1, 1), :]
        dend = jnp.exp(a_last - ac) * dtc
        cdec = jnp.exp(a_last)
        act = jnp.concatenate([ac, ac], axis=0).T
        dtt = jnp.concatenate([dtc, dtc], axis=0).T
        for g in range(SSD_GROUPS):
            bm = xbc_s[rows, D_BRANCH + g * SSD_STATE:D_BRANCH + (g + 1) * SSD_STATE]
            cm = xbc_s[rows, D_BRANCH + (SSD_GROUPS + g) * SSD_STATE:D_BRANCH + (SSD_GROUPS + g + 1) * SSD_STATE]
            bmb = _bf(bm)
            cb2 = _dot_nt(_bf(cm), jnp.concatenate([bmb, bmb], axis=0))
            for pp in range(npairs // SSD_GROUPS):
                pr = g * (npairs // SSD_GROUPS) + pp
                h0, h1 = 2 * pr, 2 * pr + 1
                xp = xbc_s[rows, pr * LANES:(pr + 1) * LANES]
                acol = jnp.where(lo, jnp.broadcast_to(ac[:, h0:h0 + 1], (CHUNK, LANES)),
                                 jnp.broadcast_to(ac[:, h1:h1 + 1], (CHUNK, LANES)))
                arow = jnp.where(lo, act[h0:h0 + 1, :], act[h1:h1 + 1, :])
                dtrow = jnp.where(lo, dtt[h0:h0 + 1, :], dtt[h1:h1 + 1, :])
                seg = jnp.where(tri2, jnp.exp(jnp.where(tri2, acol - arow, 0.0)), 0.0)
                m = cb2 * seg * dtrow
                xbd = jnp.concatenate([xp * m_lo, xp * m_hi], axis=0)
                y = _dot(_bf(m), _bf(xbd))
                eac = jnp.exp(acol)
                cs = jnp.concatenate([cm * jnp.broadcast_to(eac[:, 0:1], (CHUNK, LANES)),
                                      cm * jnp.broadcast_to(eac[:, LANES - 1:LANES], (CHUNK, LANES))], axis=1)
                st = st_s[pr]
                y = y + _dot(_bf(cs), _bf(st))
                y = y + dsk_ref[:, pr * LANES:(pr + 1) * LANES] * xp
                y_s[pr, rows, :] = y
                dcol = jnp.where(lo, jnp.broadcast_to(dend[:, h0:h0 + 1], (CHUNK, LANES)),
                                 jnp.broadcast_to(dend[:, h1:h1 + 1], (CHUNK, LANES)))
                snew = _dot_tn(bmb, _bf(xp * dcol))
                cdrow = jnp.where(lo, jnp.broadcast_to(cdec[:, h0:h0 + 1], (1, LANES)),
                                  jnp.broadcast_to(cdec[:, h1:h1 + 1], (1, LANES)))
                st_s[pr] = st * cdrow + jnp.concatenate([snew * m_lo, snew * m_hi], axis=0)
        tick()

    y = jnp.concatenate([y_s[pr] for pr in range(npairs)], axis=1) * _silu(body[:, 0:D_BRANCH])
    _keep_tail(pd_ref)
    y = y * lax.rsqrt(jnp.mean(y * y, axis=-1, keepdims=True) + EPS) * nw_ref[...]
    ycat_ref[:, 3 * D_BRANCH:4 * D_BRANCH] = _bf(y)


class _Interleave:
    def __init__(self):
        self.queue = []
        self.per_tick = 1

    def load(self, thunks, nticks):
        self.flush()
        self.queue = list(thunks)
        self.per_tick = -(-len(self.queue) // nticks)

    def tick(self):
        for _ in range(self.per_tick):
            if self.queue:
                self.queue.pop(0)()

    def flush(self):
        while self.queue:
            self.queue.pop(0)()


PROJ_CHUNK = 256


def _layer_kernel(final, tiles_per_seq,
                  x_ref, nw_ref, win_ref, wind_ref,
                  a_cw, a_cb, a_wg, a_bg, a_l,
                  b_toep, b_eb, b_f, b_pw, b_d, b_gw, b_gb,
                  c_wg, c_bg, c_nw,
                  d_cw, d_cb, d_dtb, d_alog, d_dsk, d_nw,
                  wout_ref, nf_ref,
                  o_ref,
                  pa_s, pb_s, pc_s, pd_s, xs_s, hb_s,
                  ycat, buf0, buf1, p2_s, l2_s, c2_s, lru_carry, s5_carry,
                  qd_s, ki_s, ke_s, gl_s, gla_st, xbc_s, ac_s, dt_s, ssd_st):
    g = pl.program_id(0)

    @pl.when(g == 0)
    def _():
        pa_s[...] = jnp.zeros_like(pa_s)
        pb_s[...] = jnp.zeros_like(pb_s)
        pc_s[...] = jnp.zeros_like(pc_s)
        pd_s[...] = jnp.zeros_like(pd_s)
        xs_s[...] = jnp.zeros_like(xs_s)
        hb_s[...] = jnp.zeros_like(hb_s)

    @pl.when((g == 0) | (lax.rem(g - 1, tiles_per_seq) == 0))
    def _():
        pa_s[0:SUBLANES, :] = jnp.zeros((SUBLANES, SEG_B - SEG_A), F32)
        pd_s[0:SUBLANES, :] = jnp.zeros((SUBLANES, SEG_D_W), F32)
        lru_carry[...] = jnp.zeros_like(lru_carry)
        s5_carry[...] = jnp.zeros_like(s5_carry)
        gla_st[...] = jnp.zeros_like(gla_st)
        ssd_st[...] = jnp.zeros_like(ssd_st)

    def proj_chunks(w_ref, w_off, dst_ref, row0, width):
        out = []
        for c0 in range(0, width, PROJ_CHUNK):
            c1 = min(c0 + PROJ_CHUNK, width)

            def thunk(c0=c0, c1=c1):
                dst_ref[row0:row0 + TILE, c0:c1] = _dot(hb_s[...], w_ref[:, w_off + c0:w_off + c1])
            out.append(thunk)
        return out

    def out_chunks(m):
        out = []
        rows = slice(m * D_BRANCH, (m + 1) * D_BRANCH)
        for c0 in range(0, D_MODEL, PROJ_CHUNK):
            cols = slice(c0, c0 + PROJ_CHUNK)

            def thunk(cols=cols):
                base = xs_s[:, cols] if m == 0 else o_ref[:, cols]
                o_ref[:, cols] = base + _dot(ycat[:, rows], wout_ref[rows, cols])
            out.append(thunk)
        return out

    il = _Interleave()
    il.load(proj_chunks(wind_ref, 0, pd_s, SUBLANES, SEG_D_W), 6)
    _mixer_lru(pa_s, ycat, buf0, buf1, p2_s, l2_s, c2_s, lru_carry, a_cw, a_cb, a_wg, a_bg, a_l, il.tick)
    il.flush()
    hb_s[...] = _bf(_rmsnorm(x_ref[...], nw_ref[...]))
    il.load(proj_chunks(win_ref, SEG_A, pa_s, SUBLANES, SEG_B - SEG_A) + out_chunks(0), 5)
    _mixer_s5(pb_s, ycat, buf0, buf1, s5_carry, b_toep, b_eb, b_f, b_pw, b_d, b_gw, b_gb, il.tick)
    il.load(proj_chunks(win_ref, SEG_B, pb_s, 0, SEG_C - SEG_B) + out_chunks(1), NCHUNK + 2)
    _mixer_gla(pc_s, ycat, qd_s, ki_s, ke_s, gl_s, buf1, gla_st, c_wg, c_bg, c_nw, il.tick)
    il.load(proj_chunks(win_ref, SEG_C, pc_s, 0, SEG_C_W) + out_chunks(2), NCHUNK + 2)
    _mixer_ssd(pd_s, ycat, xbc_s, ac_s, dt_s, buf1, ssd_st, d_cw, d_cb, d_dtb, d_alog, d_dsk, d_nw, il.tick)
    il.load(out_chunks(3), 1)
    il.flush()
    if final:
        o_ref[...] = _rmsnorm(o_ref[...], nf_ref[...])
    xs_s[...] = x_ref[...]


def _layer_call(layer, final, x, consts):
    bsz, seq, _ = x.shape
    assert seq % TILE == 0
    tiles_per_seq = seq // TILE
    ntiles = bsz * tiles_per_seq

    def wspec(arr):
        nd = arr.ndim - 1
        return pl.BlockSpec((None,) + arr.shape[1:], lambda g, _n=nd: (layer,) + (0,) * _n,
                            pipeline_mode=pl.Buffered(1))

    def cspec(arr):
        nd = arr.ndim
        return pl.BlockSpec(arr.shape, lambda g, _n=nd: (0,) * _n, pipeline_mode=pl.Buffered(1))

    def tile_index(t):
        return (t // tiles_per_seq, t % tiles_per_seq, 0)

    per_layer = consts["per_layer"]
    shared = consts["shared"]
    in_x = pl.BlockSpec((None, TILE, D_MODEL), lambda g: tile_index(jnp.minimum(g, ntiles - 1)))
    out_x = pl.BlockSpec((None, TILE, D_MODEL), lambda g: tile_index(jnp.maximum(g - 1, 0)))
    scratch = [
        pltpu.VMEM((TILE + SUBLANES, SEG_B - SEG_A), F32),
        pltpu.VMEM((TILE, SEG_C - SEG_B), F32),
        pltpu.VMEM((TILE, SEG_C_W), F32),
        pltpu.VMEM((TILE + SUBLANES, SEG_D_W), F32),
        pltpu.VMEM((TILE, D_MODEL), F32),
        pltpu.VMEM((TILE, D_MODEL), BF16),
        pltpu.VMEM((TILE, 4 * D_BRANCH), BF16),
        pltpu.VMEM((4, TILE, LANES), F32),
        pltpu.VMEM((4, TILE, LANES), F32),
        pltpu.VMEM((4, TILE // 8, LANES), F32),
        pltpu.VMEM((4, TILE // 8, LANES), F32),
        pltpu.VMEM((4, TILE // 8, LANES), F32),
        pltpu.VMEM((SUBLANES, D_BRANCH), F32),
        pltpu.VMEM((S5_SLABS, SUBLANES, S5_SW), F32),
        pltpu.VMEM((TILE, GLA_DK), BF16),
        pltpu.VMEM((TILE, GLA_DK), BF16),
        pltpu.VMEM((TILE, GLA_DK), BF16),
        pltpu.VMEM((NCHUNK * SUBLANES, GLA_DK), F32),
        pltpu.VMEM((GLA_HEADS, GLA_HV, LANES), F32),
        pltpu.VMEM((TILE, 2 * D_BRANCH), F32),
        pltpu.VMEM((TILE, LANES), F32),
        pltpu.VMEM((TILE, LANES), F32),
        pltpu.VMEM((SSD_HEADS // 2, 2 * SSD_STATE, LANES), F32),
    ]
    return pl.pallas_call(
        functools.partial(_layer_kernel, final, tiles_per_seq),
        grid=(ntiles + 1,),
        in_specs=[in_x] + [wspec(a) for a in per_layer] + [cspec(a) for a in shared],
        out_specs=out_x,
        out_shape=jax.ShapeDtypeStruct(x.shape, F32),
        scratch_shapes=scratch,
        compiler_params=pltpu.CompilerParams(
            dimension_semantics=("arbitrary",),
            vmem_limit_bytes=VMEM_LIMIT_BYTES),
        name=f"layer{layer}",
    )(x, *per_layer, *shared)


def _prepare(norm_w, w_in, lru_conv_w, lru_conv_b, lru_w_r, lru_b_r, lru_w_i, lru_b_i, lru_l,
             s5_lam_re, s5_lam_im, s5_log_dt, s5_b_re, s5_b_im, s5_c_re, s5_c_im, s5_d, s5_glu_w, s5_glu_b,
             gla_w_gate, gla_b_gate, gla_norm_w,
             ssd_conv_w, ssd_conv_b, ssd_dt_bias, ssd_a_log, ssd_d, ssd_norm_w,
             w_out, norm_f_w):
    nl = w_in.shape[0]
    row = lambda a: a.reshape(nl, 1, -1).astype(F32)
    w_abc = jnp.pad(w_in[:, :, :ORIG_C_END].astype(BF16), ((0, 0), (0, 0), (0, SEG_D - ORIG_C_END)))
    w_d = jnp.pad(w_in[:, :, ORIG_C_END:].astype(BF16), ((0, 0), (0, 0), (0, SEG_D_W - (ORIG_D_IN - ORIG_C_END))))
    hph = LRU_HEADS // 2
    eye = jnp.eye(hph, dtype=F32)

    def bd(w):
        w = w.reshape(nl, 2, hph, LRU_HDIM, LRU_HDIM)
        return jnp.einsum("lfhij,hk->lfhikj", w, eye).reshape(nl, 2, hph * LRU_HDIM, hph * LRU_HDIM)

    a_wg = jnp.concatenate([bd(lru_w_r), bd(lru_w_i)], axis=-1).astype(BF16)
    half = D_BRANCH // 2
    br = lru_b_r.reshape(nl, 2, half)
    bi = lru_b_i.reshape(nl, 2, half)
    a_bg = jnp.concatenate([br, bi], axis=-1).reshape(nl, 1, 2 * D_BRANCH)
    toep, ebm, fm, pw = _s5_prepare(s5_lam_re, s5_lam_im, s5_log_dt, s5_b_re, s5_b_im, s5_c_re, s5_c_im)
    c_wg = jnp.concatenate([gla_w_gate, jnp.zeros((nl, LANES - GLA_RANK, GLA_DK), gla_w_gate.dtype)], axis=1).astype(BF16)
    pad_h = lambda a: jnp.concatenate([a, jnp.zeros((nl, LANES - SSD_HEADS), a.dtype)], axis=-1).reshape(nl, 1, LANES)
    per_layer = [
        row(norm_w), w_abc, w_d,
        lru_conv_w.astype(F32), row(lru_conv_b), a_wg, a_bg, row(lru_l),
        toep, ebm, fm, pw, row(s5_d), s5_glu_w.astype(BF16), row(s5_glu_b),
        c_wg, row(gla_b_gate), row(gla_norm_w),
        ssd_conv_w.astype(F32), row(ssd_conv_b), pad_h(ssd_dt_bias), pad_h(ssd_a_log),
        row(jnp.repeat(ssd_d, SSD_HDIM, axis=-1)), row(ssd_norm_w),
        w_out.astype(BF16),
    ]
    shared = [norm_f_w.reshape(1, D_MODEL).astype(F32)]
    return {"per_layer": per_layer, "shared": shared}


def kernel(x, norm_w, w_in, lru_conv_w, lru_conv_b, lru_w_r, lru_b_r, lru_w_i, lru_b_i, lru_l, s5_lam_re, s5_lam_im, s5_log_dt, s5_b_re, s5_b_im, s5_c_re, s5_c_im, s5_d, s5_glu_w, s5_glu_b, gla_w_gate, gla_b_gate, gla_norm_w, ssd_conv_w, ssd_conv_b, ssd_dt_bias, ssd_a_log, ssd_d, ssd_norm_w, w_out, norm_f_w):
    consts = _prepare(norm_w, w_in, lru_conv_w, lru_conv_b, lru_w_r, lru_b_r, lru_w_i, lru_b_i, lru_l,
                      s5_lam_re, s5_lam_im, s5_log_dt, s5_b_re, s5_b_im, s5_c_re, s5_c_im, s5_d, s5_glu_w, s5_glu_b,
                      gla_w_gate, gla_b_gate, gla_norm_w,
                      ssd_conv_w, ssd_conv_b, ssd_dt_bias, ssd_a_log, ssd_d, ssd_norm_w,
                      w_out, norm_f_w)
    nl = w_in.shape[0]
    for layer in range(nl):
        x = _layer_call(layer, layer == nl - 1, x, consts)
    return x
```

```python
import functools
import math

import jax
import jax.numpy as jnp
from jax import lax
from jax.experimental import pallas as pl
from jax.experimental.pallas import tpu as pltpu

F32 = jnp.float32
BF16 = jnp.bfloat16

D_MODEL = 1024
D_BRANCH = 512
CONV_W = 4
EPS = 1e-6
LRU_HEADS = 8
LRU_HDIM = 64
LRU_C = 8.0
S5_GROUP = 16
S5_GROUPS = 32
S5_STATE = 64
GLA_HEADS = 4
GLA_DK = 256
GLA_HK = 64
GLA_HV = 128
GLA_RANK = 16
GLA_TAU = 16.0
SSD_HEADS = 8
SSD_HDIM = 64
SSD_GROUPS = 2
SSD_STATE = 128
CHUNK = 64

LANES = 128
SUBLANES = 8
VMEM_LIMIT_BYTES = 62 * 1024 * 1024

TILE = 512
NCHUNK = TILE // CHUNK
S5_R = 4
S5_PH = 8
S5_NB = TILE // S5_R
S5_J = S5_NB // S5_PH
S5_SLABS = D_BRANCH // LANES
S5_GPS = LANES // S5_GROUP
S5_SW = S5_GPS * S5_STATE
S5_HS_STEPS = tuple(k for k in (1, 2, 4, 8, 16, 32) if k < S5_J)
PW_L1 = 0
PW_HS = S5_PH - 1
PW_CARRY = PW_HS + len(S5_HS_STEPS)
PW_ROWS = PW_CARRY + S5_J

SEG_A = 0
SEG_B = 1024
SEG_C = 2048
SEG_C_W = 1664
SEG_D = SEG_C + SEG_C_W
SEG_D_W = 1664
N_IN = SEG_D + SEG_D_W
ORIG_C_END = 3600
ORIG_D_IN = 5144


def _dot(a, b):
    return jnp.dot(a, b, preferred_element_type=F32)


def _dot_nt(a, b):
    return lax.dot_general(a, b, (((1,), (1,)), ((), ())), preferred_element_type=F32)


def _dot_tn(a, b):
    return lax.dot_general(a, b, (((0,), (0,)), ((), ())), preferred_element_type=F32)


def _bf(x):
    return x.astype(BF16)


def _softplus(x):
    return jnp.maximum(x, 0.0) + jnp.log1p(jnp.exp(-jnp.abs(x)))


def _sigmoid(x):
    return 0.5 * jnp.tanh(0.5 * x) + 0.5


def _silu(x):
    return x * _sigmoid(x)


def _gelu_tanh(x):
    c = math.sqrt(2.0 / math.pi)
    return 0.5 * x * (1.0 + jnp.tanh(c * (x + 0.044715 * (x * x * x))))


def _cmul(ar, ai, br, bi):
    return ar * br - ai * bi, ar * bi + ai * br


def _split_hi_lo(x):
    hi = x.astype(BF16)
    lo = (x - hi.astype(F32)).astype(BF16)
    return hi, lo


def _chunk_cumsum(x):
    hi, lo = _split_hi_lo(x)
    lo2 = (x - hi.astype(F32) - lo.astype(F32)).astype(BF16)
    ri = lax.broadcasted_iota(jnp.int32, (CHUNK, CHUNK), 0)
    ci = lax.broadcasted_iota(jnp.int32, (CHUNK, CHUNK), 1)
    t = (ci <= ri).astype(BF16)
    out = []
    for c in range(x.shape[0] // CHUNK):
        rows = slice(c * CHUNK, (c + 1) * CHUNK)
        out.append(_dot(t, hi[rows]) + _dot(t, lo[rows]) + _dot(t, lo2[rows]))
    return jnp.concatenate(out, axis=0)


def _s5_prep_kernel(lr_ref, li_ref, ldt_ref, lrc_ref, lic_ref, bt_re_ref, bt_im_ref, c_re_ref, c_im_ref,
                    ct_re_ref, ct_im_ref, toep_ref, eb_ref, f_ref, pw_ref):
    def discretise(lr, li, dt):
        mag = jnp.exp(lr * dt)
        ab_re = mag * jnp.cos(li * dt)
        ab_im = mag * jnp.sin(li * dt)
        den = lr * lr + li * li
        nr = ab_re - 1.0
        coef_re = (nr * lr + ab_im * li) / den
        coef_im = (ab_im * lr - nr * li) / den
        return ab_re, ab_im, coef_re, coef_im

    def powers(ab_re, ab_im, n):
        out = [(jnp.ones_like(ab_re), jnp.zeros_like(ab_re)), (ab_re, ab_im)]
        for _ in range(2, n + 1):
            out.append(_cmul(out[-1][0], out[-1][1], ab_re, ab_im))
        return out

    ab_re, ab_im, cf_re, cf_im = discretise(lr_ref[...], li_ref[...], jnp.exp(ldt_ref[...]))
    low = powers(ab_re, ab_im, S5_R)
    l1 = low[S5_R]
    l1_pows = [l1]
    for _ in range(2, S5_PH + 1):
        l1_pows.append(_cmul(l1_pows[-1][0], l1_pows[-1][1], l1[0], l1[1]))
    l2 = l1_pows[S5_PH - 1]
    for r in range(1, S5_PH):
        pw_ref[0, PW_L1 + r - 1] = l1_pows[r - 1][0]
        pw_ref[1, PW_L1 + r - 1] = l1_pows[r - 1][1]
    sq = l2
    for i, k in enumerate(S5_HS_STEPS):
        pw_ref[0, PW_HS + i] = sq[0]
        pw_ref[1, PW_HS + i] = sq[1]
        sq = _cmul(sq[0], sq[1], sq[0], sq[1])
    cur = l2
    for j in range(S5_J):
        pw_ref[0, PW_CARRY + j] = cur[0]
        pw_ref[1, PW_CARRY + j] = cur[1]
        cur = _cmul(cur[0], cur[1], l2[0], l2[1])

    toep_ref[...] = jnp.zeros_like(toep_ref)
    eb_ref[...] = jnp.zeros_like(eb_ref)
    f_ref[...] = jnp.zeros_like(f_ref)
    hi = lax.Precision.HIGHEST
    for g in range(S5_GPS):
        row_pw = [(p[0][g:g + 1, :], p[1][g:g + 1, :]) for p in low]
        bbt_re, bbt_im = _cmul(cf_re[g:g + 1, :], cf_im[g:g + 1, :], bt_re_ref[g], bt_im_ref[g])
        for t in range(S5_R):
            e_re, e_im = _cmul(row_pw[S5_R - 1 - t][0], row_pw[S5_R - 1 - t][1], bbt_re, bbt_im)
            rows = slice(t * LANES + g * S5_GROUP, t * LANES + (g + 1) * S5_GROUP)
            eb_ref[rows, g * S5_STATE:(g + 1) * S5_STATE] = _bf(e_re)
            eb_ref[rows, S5_SW + g * S5_STATE:S5_SW + (g + 1) * S5_STATE] = _bf(e_im)
        for k in range(S5_R):
            cl_re, cl_im = _cmul(c_re_ref[g], c_im_ref[g], row_pw[k][0], row_pw[k][1])
            kt = (lax.dot_general(bbt_re, cl_re, (((1,), (1,)), ((), ())), preferred_element_type=F32, precision=hi)
                  - lax.dot_general(bbt_im, cl_im, (((1,), (1,)), ((), ())), preferred_element_type=F32, precision=hi))
            ktb = _bf(kt)
            for t in range(S5_R - k):
                r = t + k
                toep_ref[t * LANES + g * S5_GROUP:t * LANES + (g + 1) * S5_GROUP,
                         r * LANES + g * S5_GROUP:r * LANES + (g + 1) * S5_GROUP] = ktb
        abc_re, abc_im, _, _ = discretise(lrc_ref[g], lic_ref[g], jnp.exp(ldt_ref[g:g + 1, :]))
        col_pw = powers(abc_re, abc_im, S5_R)
        for r in range(S5_R):
            f_re, f_im = _cmul(ct_re_ref[g], ct_im_ref[g], col_pw[r + 1][0], col_pw[r + 1][1])
            cols = slice(r * LANES + g * S5_GROUP, r * LANES + (g + 1) * S5_GROUP)
            f_ref[g * S5_STATE:(g + 1) * S5_STATE, cols] = _bf(f_re)
            f_ref[S5_SW + g * S5_STATE:S5_SW + (g + 1) * S5_STATE, cols] = _bf(-f_im)


def _s5_prepare(lam_re, lam_im, log_dt, b_re, b_im, c_re, c_im):
    nl = lam_re.shape[0]
    g8 = S5_GPS
    rl = S5_R * LANES

    def im4(l, j):
        return (l, j, 0, 0)

    def im3(l, j):
        return (l, j, 0)

    gsn = pl.BlockSpec((None, g8, S5_GROUP, S5_STATE), im4)
    toep, ebm, fm, pw = pl.pallas_call(
        _s5_prep_kernel,
        grid=(nl, S5_SLABS),
        in_specs=[
            pl.BlockSpec((None, g8, S5_STATE), im3),
            pl.BlockSpec((None, g8, S5_STATE), im3),
            pl.BlockSpec((None, g8, 1), im3),
            pl.BlockSpec((None, g8, S5_STATE, 1), im4),
            pl.BlockSpec((None, g8, S5_STATE, 1), im4),
            gsn, gsn, gsn, gsn,
            pl.BlockSpec((None, g8, S5_STATE, S5_GROUP), im4),
            pl.BlockSpec((None, g8, S5_STATE, S5_GROUP), im4),
        ],
        out_specs=[
            pl.BlockSpec((None, None, rl, rl), im4),
            pl.BlockSpec((None, None, rl, 2 * S5_SW), im4),
            pl.BlockSpec((None, None, 2 * S5_SW, rl), im4),
            pl.BlockSpec((None, None, 2, PW_ROWS, g8, S5_STATE), lambda l, j: (l, j, 0, 0, 0, 0)),
        ],
        out_shape=[
            jax.ShapeDtypeStruct((nl, S5_SLABS, rl, rl), BF16),
            jax.ShapeDtypeStruct((nl, S5_SLABS, rl, 2 * S5_SW), BF16),
            jax.ShapeDtypeStruct((nl, S5_SLABS, 2 * S5_SW, rl), BF16),
            jax.ShapeDtypeStruct((nl, S5_SLABS, 2, PW_ROWS, g8, S5_STATE), F32),
        ],
        name="s5_prepare",
    )(lam_re, lam_im, log_dt[..., None], lam_re[..., None], lam_im[..., None],
      jnp.swapaxes(b_re, -1, -2), jnp.swapaxes(b_im, -1, -2), c_re, c_im,
      jnp.swapaxes(c_re, -1, -2), jnp.swapaxes(c_im, -1, -2))
    return toep, ebm, fm, pw.reshape(nl, S5_SLABS, 2, PW_ROWS, S5_SW)


def _rmsnorm(x, w):
    return x * lax.rsqrt(jnp.mean(x * x, axis=-1, keepdims=True) + EPS) * w


NPH = TILE // SUBLANES


def _phase_conv(p_ref, tile, w_ref, b_ref, c0):
    cols = slice(c0, c0 + LANES)
    w = w_ref[:, cols]
    b = b_ref[:, cols]
    xs = {r: p_ref[tile, pl.ds(SUBLANES + r, NPH, stride=SUBLANES), :] for r in range(1 - CONV_W, SUBLANES)}
    out = []
    for r in range(SUBLANES):
        acc = b + w[CONV_W - 1:CONV_W, :] * xs[r]
        for k in range(CONV_W - 1):
            acc = acc + w[k:k + 1, :] * xs[r - (CONV_W - 1) + k]
        out.append(acc)
    return out


def _keep_tail(p_ref):
    p_ref[:, 0:SUBLANES, :] = p_ref[:, TILE:TILE + SUBLANES, :]


def _row_iota(shape):
    return lax.broadcasted_iota(jnp.int32, shape, 0)


def _lane_iota(shape):
    return lax.broadcasted_iota(jnp.int32, shape, 1)


def _lru_scan(a_s, b_s, p2_s, l2_s, c2_s, carry_ref, tick):
    n1 = TILE // 8
    n2 = n1 // 8
    assert n2 == SUBLANES
    row = _row_iota((SUBLANES, LANES))
    for k in range(D_BRANCH // LANES):
        tick()

        def ph1(ref, r):
            return ref.at[k, pl.ds(r, n1, stride=8), :]

        def ph2(ref, r):
            return ref.at[k, pl.ds(r, n2, stride=8), :]

        p = ph1(a_s, 0)[...]
        l = ph1(b_s, 0)[...]
        for r in range(1, 8):
            ar = ph1(a_s, r)[...]
            br = ph1(b_s, r)[...]
            l = ar * l + br
            p = ar * p
            ph1(a_s, r)[...] = p
            ph1(b_s, r)[...] = l
        p2_s[k] = p
        l2_s[k] = l
        p2 = [ph2(p2_s, 0)[...]]
        l2 = [ph2(l2_s, 0)[...]]
        for r in range(1, 8):
            ar = ph2(p2_s, r)[...]
            br = ph2(l2_s, r)[...]
            l2.append(ar * l2[-1] + br)
            p2.append(ar * p2[-1])
        p3, l3 = p2[7], l2[7]
        for sh in (1, 2, 4):
            psh = jnp.where(row >= sh, pltpu.roll(p3, sh, 0), 1.0)
            lsh = jnp.where(row >= sh, pltpu.roll(l3, sh, 0), 0.0)
            l3 = p3 * lsh + l3
            p3 = p3 * psh
        cin = carry_ref[:, k * LANES:(k + 1) * LANES]
        c3 = l3 + p3 * cin
        c3prev = jnp.where(row >= 1, pltpu.roll(c3, 1, 0), cin)
        carry_ref[:, k * LANES:(k + 1) * LANES] = jnp.broadcast_to(c3[SUBLANES - 1:SUBLANES, :], (SUBLANES, LANES))
        ph2(c2_s, 0)[...] = c3prev
        for r in range(1, 8):
            ph2(c2_s, r)[...] = l2[r - 1] + p2[r - 1] * c3prev
        cprev = c2_s[k]
        for r in range(8):
            ph1(b_s, r)[...] = ph1(b_s, r)[...] + ph1(a_s, r)[...] * cprev


def _mixer_lru(pa_ref, ycat_ref, a_s, b_s, p2_s, l2_s, c2_s, carry_ref,
               cw_ref, cb_ref, wg_ref, bg_ref, ll_ref, tick):
    tick()
    ntile = D_BRANCH // LANES
    for k in range(ntile):
        for r, ur in enumerate(_phase_conv(pa_ref, k, cw_ref, cb_ref, k * LANES)):
            a_s[k, pl.ds(r, NPH, stride=SUBLANES), :] = ur
    u = jnp.concatenate([a_s[k] for k in range(ntile)], axis=1)
    ub = _bf(u)
    half = D_BRANCH // 2
    rs, is_ = [], []
    for hf in range(2):
        lg = _dot(ub[:, hf * half:(hf + 1) * half], wg_ref[hf]) + bg_ref[:, hf * D_BRANCH:(hf + 1) * D_BRANCH]
        tick()
        rs.append(_sigmoid(lg[:, 0:half]))
        is_.append(_sigmoid(lg[:, half:2 * half]))
    r = jnp.concatenate(rs, axis=1)
    i = jnp.concatenate(is_, axis=1)
    coef = -LRU_C * _softplus(-ll_ref[...])
    a = jnp.exp(coef * r)
    om = 1.0 - a * a
    mult = jnp.where(om > 0.0, om * lax.rsqrt(om), 0.0)
    b = mult * i * u
    for k in range(D_BRANCH // LANES):
        a_s[k] = a[:, k * LANES:(k + 1) * LANES]
        b_s[k] = b[:, k * LANES:(k + 1) * LANES]
    _lru_scan(a_s, b_s, p2_s, l2_s, c2_s, carry_ref, tick)
    for k in range(D_BRANCH // LANES):
        ycat_ref[:, k * LANES:(k + 1) * LANES] = _bf(
            b_s[k] * _silu(pa_ref[ntile + k, SUBLANES:SUBLANES + TILE, :]))
    _keep_tail(pa_ref)


def _shift_rows(c, k, row):
    n = c.shape[0]
    if k % SUBLANES == 0:
        return jnp.concatenate([jnp.zeros((k, c.shape[1]), c.dtype), c[:n - k]], axis=0)
    return jnp.where(row >= k, pltpu.roll(c, k, 0), 0.0)


def _mixer_s5(pb_ref, ycat_ref, u_s, y_s, carry_ref, toep_ref, eb_ref, f_ref, pw_ref, d_ref, gw_ref, gb_ref, tick):
    for j in range(S5_SLABS):
        u_s[j] = pb_ref[:, j * LANES:(j + 1) * LANES]
    J = S5_J
    row = _row_iota((J, S5_SW))
    span = S5_PH * S5_R
    for j in range(S5_SLABS):
        lhs = jnp.concatenate(
            [jnp.concatenate([u_s[j, pl.ds(S5_R * ph + t, J, stride=span), :] for t in range(S5_R)], axis=1)
             for ph in range(S5_PH)], axis=0)
        lhs = _bf(lhs)
        y_loc = _dot(lhs, toep_ref[j])
        e = _dot(lhs, eb_ref[j])
        tick()

        def pw(i):
            return pw_ref[j, 0, i:i + 1, :], pw_ref[j, 1, i:i + 1, :]

        l1r, l1i = pw(PW_L1)
        lr_ = [e[0:J, 0:S5_SW]]
        li_ = [e[0:J, S5_SW:2 * S5_SW]]
        for ph in range(1, S5_PH):
            mr, mi = _cmul(lr_[-1], li_[-1], l1r, l1i)
            lr_.append(mr + e[ph * J:(ph + 1) * J, 0:S5_SW])
            li_.append(mi + e[ph * J:(ph + 1) * J, S5_SW:2 * S5_SW])
        cr, ci = lr_[-1], li_[-1]
        for idx, k in enumerate(S5_HS_STEPS):
            qr, qi = pw(PW_HS + idx)
            mr, mi = _cmul(_shift_rows(cr, k, row), _shift_rows(ci, k, row), qr, qi)
            cr, ci = cr + mr, ci + mi
        cin_r = carry_ref[j, 0:1, :]
        cin_i = carry_ref[j, 1:2, :]
        tr = pw_ref[j, 0, PW_CARRY:PW_CARRY + J, :]
        ti = pw_ref[j, 1, PW_CARRY:PW_CARRY + J, :]
        mr, mi = _cmul(tr, ti, cin_r, cin_i)
        cr, ci = cr + mr, ci + mi
        carry_ref[j, 0:1, :] = cr[J - 1:J, :]
        carry_ref[j, 1:2, :] = ci[J - 1:J, :]
        cpr = jnp.where(row >= 1, pltpu.roll(cr, 1, 0), cin_r)
        cpi = jnp.where(row >= 1, pltpu.roll(ci, 1, 0), cin_i)
        sp_r, sp_i = [cpr], [cpi]
        for ph in range(1, S5_PH):
            qr, qi = pw(PW_L1 + ph - 1)
            mr, mi = _cmul(cpr, cpi, qr, qi)
            sp_r.append(lr_[ph - 1] + mr)
            sp_i.append(li_[ph - 1] + mi)
        sprev = jnp.concatenate([jnp.concatenate(sp_r, axis=0), jnp.concatenate(sp_i, axis=0)], axis=1)
        y = y_loc + _dot(_bf(sprev), f_ref[j])
        for ph in range(S5_PH):
            for t in range(S5_R):
                y_s[j, pl.ds(S5_R * ph + t, J, stride=span), :] = y[ph * J:(ph + 1) * J, t * LANES:(t + 1) * LANES]
    tick(2)
    yb = jnp.concatenate([y_s[j] for j in range(S5_SLABS)], axis=1) + d_ref[...] * pb_ref[:, 0:D_BRANCH]
    yb = _gelu_tanh(yb)
    glu = _dot(_bf(yb), gw_ref[...])
    tick(2)
    yb = yb * _sigmoid(glu + gb_ref[...])
    ycat_ref[:, D_BRANCH:2 * D_BRANCH] = _bf(yb * _silu(pb_ref[:, D_BRANCH:2 * D_BRANCH]))


def _mixer_gla(pc_ref, ycat_ref, qd_s, ki_s, ke_s, gl_s, o_s, st_s, wg_ref, bg_ref, nw_ref, tick):
    k = pc_ref[:, GLA_DK:2 * GLA_DK]
    z_off = 2 * GLA_DK + D_BRANCH
    glow = pc_ref[:, 2 * GLA_DK + 2 * D_BRANCH:SEG_C_W]
    logits = _dot(_bf(glow), wg_ref[...]) + bg_ref[...]
    tick()
    g = -_softplus(-logits) / GLA_TAU
    gc = _chunk_cumsum(g)
    tick(2)
    qd_s[...] = _bf(pc_ref[:, 0:GLA_DK] * (GLA_HK ** -0.5) * jnp.exp(gc))
    ki_s[...] = _bf(k * jnp.exp(-gc))
    gl = jnp.concatenate(
        [jnp.broadcast_to(gc[c * CHUNK + CHUNK - 1:c * CHUNK + CHUNK, :], (CHUNK, GLA_DK)) for c in range(NCHUNK)], axis=0)
    ke_s[...] = _bf(k * jnp.exp(gl - gc))
    for c in range(NCHUNK):
        gl_s[c * SUBLANES:(c + 1) * SUBLANES, :] = jnp.exp(gl[c * CHUNK:c * CHUNK + SUBLANES, :])

    lane = _lane_iota((1, LANES))
    m_lo = (lane < GLA_HK).astype(BF16)
    m_hi = (lane >= GLA_HK).astype(BF16)
    tri = (_lane_iota((CHUNK, CHUNK)) <= _row_iota((CHUNK, CHUNK)))

    for c in range(NCHUNK):
        if c % 2 == 0:
            tick()
        r0 = c * CHUNK
        rows = pl.ds(r0, CHUNK)
        for tl in range(GLA_DK // LANES):
            cols = slice(tl * LANES, (tl + 1) * LANES)
            qd = qd_s[rows, cols]
            ki = ki_s[rows, cols]
            ke = ke_s[rows, cols]
            dec = gl_s[pl.ds(c * SUBLANES, 1), cols]
            qstack = jnp.concatenate([qd * m_lo, qd * m_hi], axis=0)
            s2 = _dot_nt(qstack, ki)
            for hh in range(2):
                h = 2 * tl + hh
                vh = _bf(pc_ref[rows, 2 * GLA_DK + h * GLA_HV:2 * GLA_DK + (h + 1) * GLA_HV])
                sc = jnp.where(tri, s2[hh * CHUNK:(hh + 1) * CHUNK, :], 0.0)
                qh = qstack[hh * CHUNK:(hh + 1) * CHUNK, :]
                st = st_s[h]
                o = _dot(_bf(sc), vh) + _dot_nt(qh, _bf(st))
                o_s[h, rows, :] = o
                st_s[h] = st * dec + _dot_tn(vh, ke)

    tick()
    for h in range(GLA_HEADS):
        o = o_s[h]
        o = o * lax.rsqrt(jnp.mean(o * o, axis=-1, keepdims=True) + EPS) * nw_ref[...]
        zh = pc_ref[:, z_off + h * GLA_HV:z_off + (h + 1) * GLA_HV]
        ycat_ref[:, 2 * D_BRANCH + h * GLA_HV:2 * D_BRANCH + (h + 1) * GLA_HV] = _bf(o * _silu(zh))


def _mixer_ssd(pd_ref, ycat_ref, xbc_s, ac_s, dt_s, y_s, st_s,
               cw_ref, cb_ref, dtb_ref, alog_ref, dsk_ref, nw_ref, tick):
    zt = D_BRANCH // LANES
    xt = 2 * D_BRANCH // LANES
    body = slice(SUBLANES, SUBLANES + TILE)
    tick(2)
    for k in range(xt):
        for r, ur in enumerate(_phase_conv(pd_ref, zt + k, cw_ref, cb_ref, k * LANES)):
            xbc_s[k, pl.ds(r, NPH, stride=SUBLANES), :] = _silu(ur)
    dt = _softplus(pd_ref[zt + xt, body, :] + dtb_ref[...])
    a = -jnp.exp(alog_ref[...])
    dt_s[...] = dt
    ac_s[...] = _chunk_cumsum(dt * a)
    tick()

    lane = _lane_iota((1, LANES))
    lo = lane < SSD_HDIM
    m_lo = lo.astype(F32)
    m_hi = 1.0 - m_lo
    lane_j = jnp.where(lo, lane, lane - SSD_HDIM)
    tri2 = lane_j <= _row_iota((CHUNK, LANES))
    npairs = SSD_HEADS // 2
    ppg = npairs // SSD_GROUPS

    for c in range(NCHUNK):
        tick()
        r0 = c * CHUNK
        rows = pl.ds(r0, CHUNK)
        ac = ac_s[rows, :]
        dtc = dt_s[rows, :]
        a_last = ac_s[pl.ds(r0 + CHUNK - 1, 1), :]
        dend = jnp.exp(a_last - ac) * dtc
        cdec = jnp.exp(a_last)
        act = jnp.concatenate([ac, ac], axis=0).T
        dtt = jnp.concatenate([dtc, dtc], axis=0).T
        for g in range(SSD_GROUPS):
            bm = xbc_s[npairs + g, rows, :]
            cm = xbc_s[npairs + SSD_GROUPS + g, rows, :]
            bmb = _bf(bm)
            cb2 = _dot_nt(_bf(cm), jnp.concatenate([bmb, bmb], axis=0))
            st = st_s[g]
            y_off = _dot(_bf(cm), _bf(st))
            xd, cdrow = [], []
            for pp in range(ppg):
                pr = g * ppg + pp
                h0, h1 = 2 * pr, 2 * pr + 1
                xp = xbc_s[pr, rows, :]
                acol = jnp.where(lo, jnp.broadcast_to(ac[:, h0:h0 + 1], (CHUNK, LANES)),
                                 jnp.broadcast_to(ac[:, h1:h1 + 1], (CHUNK, LANES)))
                arow = jnp.where(lo, act[h0:h0 + 1, :], act[h1:h1 + 1, :])
                dtrow = jnp.where(lo, dtt[h0:h0 + 1, :], dtt[h1:h1 + 1, :])
                seg = jnp.where(tri2, jnp.exp(jnp.where(tri2, acol - arow, 0.0)), 0.0)
                m = cb2 * seg * dtrow
                xbd = jnp.concatenate([xp * m_lo, xp * m_hi], axis=0)
                y = _dot(_bf(m), _bf(xbd))
                y = y + y_off[:, pp * LANES:(pp + 1) * LANES] * jnp.exp(acol)
                y = y + dsk_ref[:, pr * LANES:(pr + 1) * LANES] * xp
                y_s[pr, rows, :] = y
                dcol = jnp.where(lo, jnp.broadcast_to(dend[:, h0:h0 + 1], (CHUNK, LANES)),
                                 jnp.broadcast_to(dend[:, h1:h1 + 1], (CHUNK, LANES)))
                xd.append(xp * dcol)
                cdrow.append(jnp.where(lo, jnp.broadcast_to(cdec[:, h0:h0 + 1], (1, LANES)),
                                       jnp.broadcast_to(cdec[:, h1:h1 + 1], (1, LANES))))
            snew = _dot_tn(bmb, _bf(jnp.concatenate(xd, axis=1)))
            st_s[g] = st * jnp.concatenate(cdrow, axis=1) + snew

    z = jnp.concatenate([pd_ref[k, body, :] for k in range(zt)], axis=1)
    y = jnp.concatenate([y_s[pr] for pr in range(npairs)], axis=1) * _silu(z)
    _keep_tail(pd_ref)
    y = y * lax.rsqrt(jnp.mean(y * y, axis=-1, keepdims=True) + EPS) * nw_ref[...]
    ycat_ref[:, 3 * D_BRANCH:4 * D_BRANCH] = _bf(y)


class _Interleave:
    def __init__(self):
        self.queue = []

    def load(self, thunks):
        self.flush()
        self.queue = list(thunks)

    def tick(self, n=1):
        for _ in range(n):
            if self.queue:
                self.queue.pop(0)()

    def flush(self):
        while self.queue:
            self.queue.pop(0)()


PROJ_CHUNK = 256


def _layer_kernel(final, tiles_per_seq,
                  x_ref, nw_ref, win_ref, wind_ref,
                  a_cw, a_cb, a_wg, a_bg, a_l,
                  b_toep, b_eb, b_f, b_pw, b_d, b_gw, b_gb,
                  c_wg, c_bg, c_nw,
                  d_cw, d_cb, d_dtb, d_alog, d_dsk, d_nw,
                  wout_ref, nf_ref,
                  o_ref,
                  pa_s, pb_s, pc_s, pd_s, xs_s, hb_s,
                  ycat, buf0, buf1, p2_s, l2_s, c2_s, lru_carry, s5_carry,
                  qd_s, ki_s, ke_s, gl_s, gla_st, xbc_s, ac_s, dt_s, ssd_st):
    g = pl.program_id(0)

    @pl.when(g == 0)
    def _():
        pa_s[...] = jnp.zeros_like(pa_s)
        pb_s[...] = jnp.zeros_like(pb_s)
        pc_s[...] = jnp.zeros_like(pc_s)
        pd_s[...] = jnp.zeros_like(pd_s)
        xs_s[...] = jnp.zeros_like(xs_s)
        hb_s[...] = jnp.zeros_like(hb_s)

    @pl.when((g == 0) | (lax.rem(g - 1, tiles_per_seq) == 0))
    def _():
        pa_s[:, 0:SUBLANES, :] = jnp.zeros((pa_s.shape[0], SUBLANES, LANES), F32)
        pd_s[:, 0:SUBLANES, :] = jnp.zeros((pd_s.shape[0], SUBLANES, LANES), F32)
        lru_carry[...] = jnp.zeros_like(lru_carry)
        s5_carry[...] = jnp.zeros_like(s5_carry)
        gla_st[...] = jnp.zeros_like(gla_st)
        ssd_st[...] = jnp.zeros_like(ssd_st)

    def proj_chunks(w_ref, w_off, dst_ref, width, tiled):
        out = []
        for c0 in range(0, width, PROJ_CHUNK):
            c1 = min(c0 + PROJ_CHUNK, width)

            def thunk(c0=c0, c1=c1):
                res = _dot(hb_s[...], w_ref[:, w_off + c0:w_off + c1])
                if tiled:
                    for i in range((c1 - c0) // LANES):
                        dst_ref[c0 // LANES + i, SUBLANES:SUBLANES + TILE, :] = res[:, i * LANES:(i + 1) * LANES]
                else:
                    dst_ref[:, c0:c1] = res
            out.append(thunk)
        return out

    def out_chunks(m):
        out = []
        rows = slice(m * D_BRANCH, (m + 1) * D_BRANCH)
        for c0 in range(0, D_MODEL, PROJ_CHUNK):
            cols = slice(c0, c0 + PROJ_CHUNK)

            def thunk(cols=cols):
                base = xs_s[:, cols] if m == 0 else o_ref[:, cols]
                o_ref[:, cols] = base + _dot(ycat[:, rows], wout_ref[rows, cols])
            out.append(thunk)
        return out

    il = _Interleave()
    il.load(proj_chunks(wind_ref, 0, pd_s, SEG_D_W, True))
    _mixer_lru(pa_s, ycat, buf0, buf1, p2_s, l2_s, c2_s, lru_carry, a_cw, a_cb, a_wg, a_bg, a_l, il.tick)
    il.flush()
    hb_s[...] = _bf(_rmsnorm(x_ref[...], nw_ref[...]))
    il.load(proj_chunks(win_ref, SEG_A, pa_s, SEG_B - SEG_A, True) + out_chunks(0))
    _mixer_s5(pb_s, ycat, buf0, buf1, s5_carry, b_toep, b_eb, b_f, b_pw, b_d, b_gw, b_gb, il.tick)
    il.load(proj_chunks(win_ref, SEG_B, pb_s, SEG_C - SEG_B, False) + out_chunks(1))
    _mixer_gla(pc_s, ycat, qd_s, ki_s, ke_s, gl_s, buf1, gla_st, c_wg, c_bg, c_nw, il.tick)
    il.load(proj_chunks(win_ref, SEG_C, pc_s, SEG_C_W, False) + out_chunks(2))
    _mixer_ssd(pd_s, ycat, xbc_s, ac_s, dt_s, buf1, ssd_st, d_cw, d_cb, d_dtb, d_alog, d_dsk, d_nw, il.tick)
    il.load(out_chunks(3))
    il.flush()
    if final:
        o_ref[...] = _rmsnorm(o_ref[...], nf_ref[...])
    xs_s[...] = x_ref[...]


def _layer_call(layer, final, x, consts):
    bsz, seq, _ = x.shape
    assert seq % TILE == 0
    tiles_per_seq = seq // TILE
    ntiles = bsz * tiles_per_seq

    def wspec(arr):
        nd = arr.ndim - 1
        return pl.BlockSpec((None,) + arr.shape[1:], lambda g, _n=nd: (layer,) + (0,) * _n,
                            pipeline_mode=pl.Buffered(1))

    def cspec(arr):
        nd = arr.ndim
        return pl.BlockSpec(arr.shape, lambda g, _n=nd: (0,) * _n, pipeline_mode=pl.Buffered(1))

    def tile_index(t):
        return (t // tiles_per_seq, t % tiles_per_seq, 0)

    per_layer = consts["per_layer"]
    shared = consts["shared"]
    in_x = pl.BlockSpec((None, TILE, D_MODEL), lambda g: tile_index(jnp.minimum(g, ntiles - 1)))
    out_x = pl.BlockSpec((None, TILE, D_MODEL), lambda g: tile_index(jnp.maximum(g - 1, 0)))
    scratch = [
        pltpu.VMEM(((SEG_B - SEG_A) // LANES, SUBLANES + TILE, LANES), F32),
        pltpu.VMEM((TILE, SEG_C - SEG_B), F32),
        pltpu.VMEM((TILE, SEG_C_W), F32),
        pltpu.VMEM((SEG_D_W // LANES, SUBLANES + TILE, LANES), F32),
        pltpu.VMEM((TILE, D_MODEL), F32),
        pltpu.VMEM((TILE, D_MODEL), BF16),
        pltpu.VMEM((TILE, 4 * D_BRANCH), BF16),
        pltpu.VMEM((4, TILE, LANES), F32),
        pltpu.VMEM((4, TILE, LANES), F32),
        pltpu.VMEM((4, TILE // 8, LANES), F32),
        pltpu.VMEM((4, TILE // 8, LANES), F32),
        pltpu.VMEM((4, TILE // 8, LANES), F32),
        pltpu.VMEM((SUBLANES, D_BRANCH), F32),
        pltpu.VMEM((S5_SLABS, SUBLANES, S5_SW), F32),
        pltpu.VMEM((TILE, GLA_DK), BF16),
        pltpu.VMEM((TILE, GLA_DK), BF16),
        pltpu.VMEM((TILE, GLA_DK), BF16),
        pltpu.VMEM((NCHUNK * SUBLANES, GLA_DK), F32),
        pltpu.VMEM((GLA_HEADS, GLA_HV, LANES), F32),
        pltpu.VMEM((2 * D_BRANCH // LANES, TILE, LANES), F32),
        pltpu.VMEM((TILE, LANES), F32),
        pltpu.VMEM((TILE, LANES), F32),
        pltpu.VMEM((SSD_GROUPS, SSD_STATE, D_BRANCH // SSD_GROUPS), F32),
    ]
    return pl.pallas_call(
        functools.partial(_layer_kernel, final, tiles_per_seq),
        grid=(ntiles + 1,),
        in_specs=[in_x] + [wspec(a) for a in per_layer] + [cspec(a) for a in shared],
        out_specs=out_x,
        out_shape=jax.ShapeDtypeStruct(x.shape, F32),
        scratch_shapes=scratch,
        compiler_params=pltpu.CompilerParams(
            dimension_semantics=("arbitrary",),
            vmem_limit_bytes=VMEM_LIMIT_BYTES),
        name=f"layer{layer}",
    )(x, *per_layer, *shared)


def _prepare(norm_w, w_in, lru_conv_w, lru_conv_b, lru_w_r, lru_b_r, lru_w_i, lru_b_i, lru_l,
             s5_lam_re, s5_lam_im, s5_log_dt, s5_b_re, s5_b_im, s5_c_re, s5_c_im, s5_d, s5_glu_w, s5_glu_b,
             gla_w_gate, gla_b_gate, gla_norm_w,
             ssd_conv_w, ssd_conv_b, ssd_dt_bias, ssd_a_log, ssd_d, ssd_norm_w,
             w_out, norm_f_w):
    nl = w_in.shape[0]
    row = lambda a: a.reshape(nl, 1, -1).astype(F32)
    w_abc = jnp.pad(w_in[:, :, :ORIG_C_END].astype(BF16), ((0, 0), (0, 0), (0, SEG_D - ORIG_C_END)))
    w_d = jnp.pad(w_in[:, :, ORIG_C_END:].astype(BF16), ((0, 0), (0, 0), (0, SEG_D_W - (ORIG_D_IN - ORIG_C_END))))
    hph = LRU_HEADS // 2
    eye = jnp.eye(hph, dtype=F32)

    def bd(w):
        w = w.reshape(nl, 2, hph, LRU_HDIM, LRU_HDIM)
        return jnp.einsum("lfhij,hk->lfhikj", w, eye).reshape(nl, 2, hph * LRU_HDIM, hph * LRU_HDIM)

    a_wg = jnp.concatenate([bd(lru_w_r), bd(lru_w_i)], axis=-1).astype(BF16)
    half = D_BRANCH // 2
    br = lru_b_r.reshape(nl, 2, half)
    bi = lru_b_i.reshape(nl, 2, half)
    a_bg = jnp.concatenate([br, bi], axis=-1).reshape(nl, 1, 2 * D_BRANCH)
    toep, ebm, fm, pw = _s5_prepare(s5_lam_re, s5_lam_im, s5_log_dt, s5_b_re, s5_b_im, s5_c_re, s5_c_im)
    c_wg = jnp.concatenate([gla_w_gate, jnp.zeros((nl, LANES - GLA_RANK, GLA_DK), gla_w_gate.dtype)], axis=1).astype(BF16)
    pad_h = lambda a: jnp.concatenate([a, jnp.zeros((nl, LANES - SSD_HEADS), a.dtype)], axis=-1).reshape(nl, 1, LANES)
    per_layer = [
        row(norm_w), w_abc, w_d,
        lru_conv_w.astype(F32), row(lru_conv_b), a_wg, a_bg, row(lru_l),
        toep, ebm, fm, pw, row(s5_d), s5_glu_w.astype(BF16), row(s5_glu_b),
        c_wg, row(gla_b_gate), row(gla_norm_w),
        ssd_conv_w.astype(F32), row(ssd_conv_b), pad_h(ssd_dt_bias), pad_h(ssd_a_log),
        row(jnp.repeat(ssd_d, SSD_HDIM, axis=-1)), row(ssd_norm_w),
        w_out.astype(BF16),
    ]
    shared = [norm_f_w.reshape(1, D_MODEL).astype(F32)]
    return {"per_layer": per_layer, "shared": shared}


def kernel(x, norm_w, w_in, lru_conv_w, lru_conv_b, lru_w_r, lru_b_r, lru_w_i, lru_b_i, lru_l, s5_lam_re, s5_lam_im, s5_log_dt, s5_b_re, s5_b_im, s5_c_re, s5_c_im, s5_d, s5_glu_w, s5_glu_b, gla_w_gate, gla_b_gate, gla_norm_w, ssd_conv_w, ssd_conv_b, ssd_dt_bias, ssd_a_log, ssd_d, ssd_norm_w, w_out, norm_f_w):
    consts = _prepare(norm_w, w_in, lru_conv_w, lru_conv_b, lru_w_r, lru_b_r, lru_w_i, lru_b_i, lru_l,
                      s5_lam_re, s5_lam_im, s5_log_dt, s5_b_re, s5_b_im, s5_c_re, s5_c_im, s5_d, s5_glu_w, s5_glu_b,
                      gla_w_gate, gla_b_gate, gla_norm_w,
                      ssd_conv_w, ssd_conv_b, ssd_dt_bias, ssd_a_log, ssd_d, ssd_norm_w,
                      w_out, norm_f_w)
    nl = w_in.shape[0]
    for layer in range(nl):
        x = _layer_call(layer, layer == nl - 1, x, consts)
    return x
```

```python
import functools
import math

import jax
import jax.numpy as jnp
from jax import lax
from jax.experimental import pallas as pl
from jax.experimental.pallas import tpu as pltpu

F32 = jnp.float32
BF16 = jnp.bfloat16

D_MODEL = 1024
D_BRANCH = 512
CONV_W = 4
EPS = 1e-6
LRU_HEADS = 8
LRU_HDIM = 64
LRU_C = 8.0
S5_GROUP = 16
S5_GROUPS = 32
S5_STATE = 64
GLA_HEADS = 4
GLA_DK = 256
GLA_HK = 64
GLA_HV = 128
GLA_RANK = 16
GLA_TAU = 16.0
SSD_HEADS = 8
SSD_HDIM = 64
SSD_GROUPS = 2
SSD_STATE = 128
CHUNK = 64

LANES = 128
SUBLANES = 8
VMEM_LIMIT_BYTES = 62 * 1024 * 1024

TILE = 512
NCHUNK = TILE // CHUNK
EW_ROWS = 32
S5_R = 4
S5_PH = 8
S5_NB = TILE // S5_R
S5_J = S5_NB // S5_PH
S5_SLABS = D_BRANCH // LANES
S5_GPS = LANES // S5_GROUP
S5_SW = S5_GPS * S5_STATE
S5_HS_STEPS = tuple(k for k in (1, 2, 4, 8, 16, 32) if k < S5_J)
PW_L1 = 0
PW_HS = S5_PH - 1
PW_CARRY = PW_HS + len(S5_HS_STEPS)
PW_ROWS = PW_CARRY + S5_J

SEG_A = 0
SEG_B = 1024
SEG_C = 2048
SEG_C_W = 1664
SEG_D = SEG_C + SEG_C_W
SEG_D_W = 1664
N_IN = SEG_D + SEG_D_W
ORIG_C_END = 3600
ORIG_D_IN = 5144


def _dot(a, b):
    return jnp.dot(a, b, preferred_element_type=F32)


def _dot_nt(a, b):
    return lax.dot_general(a, b, (((1,), (1,)), ((), ())), preferred_element_type=F32)


def _dot_tn(a, b):
    return lax.dot_general(a, b, (((0,), (0,)), ((), ())), preferred_element_type=F32)


def _bf(x):
    return x.astype(BF16)


def _softplus(x):
    return jnp.maximum(x, 0.0) + jnp.log1p(jnp.exp(-jnp.abs(x)))


def _sigmoid(x):
    return 0.5 * jnp.tanh(0.5 * x) + 0.5


def _silu(x):
    return x * _sigmoid(x)


def _gelu_tanh(x):
    c = math.sqrt(2.0 / math.pi)
    return 0.5 * x * (1.0 + jnp.tanh(c * (x + 0.044715 * (x * x * x))))


def _cmul(ar, ai, br, bi):
    return ar * br - ai * bi, ar * bi + ai * br


def _split_hi_lo(x):
    hi = x.astype(BF16)
    lo = (x - hi.astype(F32)).astype(BF16)
    return hi, lo


def _tril_ones():
    ri = lax.broadcasted_iota(jnp.int32, (CHUNK, CHUNK), 0)
    ci = lax.broadcasted_iota(jnp.int32, (CHUNK, CHUNK), 1)
    return (ci <= ri).astype(BF16)


def _chunk_cumsum(t, x):
    hi, lo = _split_hi_lo(x)
    lo2 = (x - hi.astype(F32) - lo.astype(F32)).astype(BF16)
    return _dot(t, hi) + _dot(t, lo) + _dot(t, lo2)


S5_PREP_INPUTS = 11


def _s5_prep_kernel(*refs):
    for j in range(S5_SLABS):
        groups = pl.ds(j * S5_GPS, S5_GPS)
        _s5_prep_slab(*[r.at[groups] for r in refs[:S5_PREP_INPUTS]], *[r.at[j] for r in refs[S5_PREP_INPUTS:]])


def _s5_prep_slab(lr_ref, li_ref, ldt_ref, lrc_ref, lic_ref, bt_re_ref, bt_im_ref, c_re_ref, c_im_ref,
                  ct_re_ref, ct_im_ref, toep_ref, eb_ref, f_ref, pw_ref):
    def discretise(lr, li, dt):
        mag = jnp.exp(lr * dt)
        ab_re = mag * jnp.cos(li * dt)
        ab_im = mag * jnp.sin(li * dt)
        den = lr * lr + li * li
        nr = ab_re - 1.0
        coef_re = (nr * lr + ab_im * li) / den
        coef_im = (ab_im * lr - nr * li) / den
        return ab_re, ab_im, coef_re, coef_im

    def powers(ab_re, ab_im, n):
        out = [(jnp.ones_like(ab_re), jnp.zeros_like(ab_re)), (ab_re, ab_im)]
        for _ in range(2, n + 1):
            out.append(_cmul(out[-1][0], out[-1][1], ab_re, ab_im))
        return out

    ab_re, ab_im, cf_re, cf_im = discretise(lr_ref[...], li_ref[...], jnp.exp(ldt_ref[...]))
    low = powers(ab_re, ab_im, S5_R)
    l1 = low[S5_R]
    l1_pows = [l1]
    for _ in range(2, S5_PH + 1):
        l1_pows.append(_cmul(l1_pows[-1][0], l1_pows[-1][1], l1[0], l1[1]))
    l2 = l1_pows[S5_PH - 1]
    for r in range(1, S5_PH):
        pw_ref[0, PW_L1 + r - 1] = l1_pows[r - 1][0]
        pw_ref[1, PW_L1 + r - 1] = l1_pows[r - 1][1]
    sq = l2
    for i, k in enumerate(S5_HS_STEPS):
        pw_ref[0, PW_HS + i] = sq[0]
        pw_ref[1, PW_HS + i] = sq[1]
        sq = _cmul(sq[0], sq[1], sq[0], sq[1])
    cur = l2
    for j in range(S5_J):
        pw_ref[0, PW_CARRY + j] = cur[0]
        pw_ref[1, PW_CARRY + j] = cur[1]
        cur = _cmul(cur[0], cur[1], l2[0], l2[1])

    toep_ref[...] = jnp.zeros(toep_ref.shape, BF16)
    eb_ref[...] = jnp.zeros(eb_ref.shape, BF16)
    f_ref[...] = jnp.zeros(f_ref.shape, BF16)
    hi = lax.Precision.HIGHEST
    for g in range(S5_GPS):
        row_pw = [(p[0][g:g + 1, :], p[1][g:g + 1, :]) for p in low]
        bbt_re, bbt_im = _cmul(cf_re[g:g + 1, :], cf_im[g:g + 1, :], bt_re_ref[g], bt_im_ref[g])
        for t in range(S5_R):
            e_re, e_im = _cmul(row_pw[S5_R - 1 - t][0], row_pw[S5_R - 1 - t][1], bbt_re, bbt_im)
            rows = slice(t * LANES + g * S5_GROUP, t * LANES + (g + 1) * S5_GROUP)
            eb_ref[rows, g * S5_STATE:(g + 1) * S5_STATE] = _bf(e_re)
            eb_ref[rows, S5_SW + g * S5_STATE:S5_SW + (g + 1) * S5_STATE] = _bf(e_im)
        for k in range(S5_R):
            cl_re, cl_im = _cmul(c_re_ref[g], c_im_ref[g], row_pw[k][0], row_pw[k][1])
            kt = (lax.dot_general(bbt_re, cl_re, (((1,), (1,)), ((), ())), preferred_element_type=F32, precision=hi)
                  - lax.dot_general(bbt_im, cl_im, (((1,), (1,)), ((), ())), preferred_element_type=F32, precision=hi))
            ktb = _bf(kt)
            for t in range(S5_R - k):
                r = t + k
                toep_ref[t * LANES + g * S5_GROUP:t * LANES + (g + 1) * S5_GROUP,
                         r * LANES + g * S5_GROUP:r * LANES + (g + 1) * S5_GROUP] = ktb
        abc_re, abc_im, _, _ = discretise(lrc_ref[g], lic_ref[g], jnp.exp(ldt_ref[g:g + 1, :]))
        col_pw = powers(abc_re, abc_im, S5_R)
        for r in range(S5_R):
            f_re, f_im = _cmul(ct_re_ref[g], ct_im_ref[g], col_pw[r + 1][0], col_pw[r + 1][1])
            cols = slice(r * LANES + g * S5_GROUP, r * LANES + (g + 1) * S5_GROUP)
            f_ref[g * S5_STATE:(g + 1) * S5_STATE, cols] = _bf(f_re)
            f_ref[S5_SW + g * S5_STATE:S5_SW + (g + 1) * S5_STATE, cols] = _bf(-f_im)


def _s5_prepare(lam_re, lam_im, log_dt, b_re, b_im, c_re, c_im):
    nl = lam_re.shape[0]
    g8 = S5_GPS
    rl = S5_R * LANES

    def im4(l):
        return (l, 0, 0, 0)

    def im3(l):
        return (l, 0, 0)

    ng = S5_GROUPS
    gsn = pl.BlockSpec((None, ng, S5_GROUP, S5_STATE), im4)
    toep, ebm, fm, pw = pl.pallas_call(
        _s5_prep_kernel,
        grid=(nl,),
        in_specs=[
            pl.BlockSpec((None, ng, S5_STATE), im3),
            pl.BlockSpec((None, ng, S5_STATE), im3),
            pl.BlockSpec((None, ng, 1), im3),
            pl.BlockSpec((None, ng, S5_STATE, 1), im4),
            pl.BlockSpec((None, ng, S5_STATE, 1), im4),
            gsn, gsn, gsn, gsn,
            pl.BlockSpec((None, ng, S5_STATE, S5_GROUP), im4),
            pl.BlockSpec((None, ng, S5_STATE, S5_GROUP), im4),
        ],
        out_specs=[
            pl.BlockSpec((None, S5_SLABS, rl, rl), im4),
            pl.BlockSpec((None, S5_SLABS, rl, 2 * S5_SW), im4),
            pl.BlockSpec((None, S5_SLABS, 2 * S5_SW, rl), im4),
            pl.BlockSpec((None, S5_SLABS, 2, PW_ROWS, g8, S5_STATE), lambda l: (l, 0, 0, 0, 0, 0)),
        ],
        out_shape=[
            jax.ShapeDtypeStruct((nl, S5_SLABS, rl, rl), BF16),
            jax.ShapeDtypeStruct((nl, S5_SLABS, rl, 2 * S5_SW), BF16),
            jax.ShapeDtypeStruct((nl, S5_SLABS, 2 * S5_SW, rl), BF16),
            jax.ShapeDtypeStruct((nl, S5_SLABS, 2, PW_ROWS, g8, S5_STATE), F32),
        ],
        name="s5_prepare",
    )(lam_re, lam_im, log_dt[..., None], lam_re[..., None], lam_im[..., None],
      jnp.swapaxes(b_re, -1, -2), jnp.swapaxes(b_im, -1, -2), c_re, c_im,
      jnp.swapaxes(c_re, -1, -2), jnp.swapaxes(c_im, -1, -2))
    return toep, ebm, fm, pw.reshape(nl, S5_SLABS, 2, PW_ROWS, S5_SW)


def _rmsnorm(x, w):
    return x * lax.rsqrt(jnp.mean(x * x, axis=-1, keepdims=True) + EPS) * w


NPH = TILE // SUBLANES


def _phase_conv(p_ref, tile, w_ref, b_ref, c0):
    cols = slice(c0, c0 + LANES)
    w = w_ref[:, cols]
    b = b_ref[:, cols]
    xs = {r: p_ref[tile, pl.ds(SUBLANES + r, NPH, stride=SUBLANES), :] for r in range(1 - CONV_W, SUBLANES)}
    out = []
    for r in range(SUBLANES):
        acc = b + w[CONV_W - 1:CONV_W, :] * xs[r]
        for k in range(CONV_W - 1):
            acc = acc + w[k:k + 1, :] * xs[r - (CONV_W - 1) + k]
        out.append(acc)
    return out


def _keep_tail(p_ref):
    p_ref[:, 0:SUBLANES, :] = p_ref[:, TILE:TILE + SUBLANES, :]


def _row_iota(shape):
    return lax.broadcasted_iota(jnp.int32, shape, 0)


def _lane_iota(shape):
    return lax.broadcasted_iota(jnp.int32, shape, 1)


def _lru_scan(a_s, b_s, p2_s, l2_s, c2_s, carry_ref, tick):
    n1 = TILE // 8
    n2 = n1 // 8
    assert n2 == SUBLANES
    row = _row_iota((SUBLANES, LANES))
    for k in range(D_BRANCH // LANES):
        tick()

        def ph1(ref, r):
            return ref.at[k, pl.ds(r, n1, stride=8), :]

        def ph2(ref, r):
            return ref.at[k, pl.ds(r, n2, stride=8), :]

        p = ph1(a_s, 0)[...]
        l = ph1(b_s, 0)[...]
        for r in range(1, 8):
            ar = ph1(a_s, r)[...]
            br = ph1(b_s, r)[...]
            l = ar * l + br
            p = ar * p
            ph1(a_s, r)[...] = p
            ph1(b_s, r)[...] = l
        p2_s[k] = p
        l2_s[k] = l
        p2 = [ph2(p2_s, 0)[...]]
        l2 = [ph2(l2_s, 0)[...]]
        for r in range(1, 8):
            ar = ph2(p2_s, r)[...]
            br = ph2(l2_s, r)[...]
            l2.append(ar * l2[-1] + br)
            p2.append(ar * p2[-1])
        p3, l3 = p2[7], l2[7]
        for sh in (1, 2, 4):
            psh = jnp.where(row >= sh, pltpu.roll(p3, sh, 0), 1.0)
            lsh = jnp.where(row >= sh, pltpu.roll(l3, sh, 0), 0.0)
            l3 = p3 * lsh + l3
            p3 = p3 * psh
        cin = carry_ref[:, k * LANES:(k + 1) * LANES]
        c3 = l3 + p3 * cin
        c3prev = jnp.where(row >= 1, pltpu.roll(c3, 1, 0), cin)
        carry_ref[:, k * LANES:(k + 1) * LANES] = jnp.broadcast_to(c3[SUBLANES - 1:SUBLANES, :], (SUBLANES, LANES))
        ph2(c2_s, 0)[...] = c3prev
        for r in range(1, 8):
            ph2(c2_s, r)[...] = l2[r - 1] + p2[r - 1] * c3prev
        cprev = c2_s[k]
        for r in range(8):
            ph1(b_s, r)[...] = ph1(b_s, r)[...] + ph1(a_s, r)[...] * cprev


def _mixer_lru(pa_ref, ycat_ref, a_s, b_s, p2_s, l2_s, c2_s, carry_ref,
               cw_ref, cb_ref, wg_ref, bg_ref, ll_ref, tick):
    tick()
    ntile = D_BRANCH // LANES
    for k in range(ntile):
        for r, ur in enumerate(_phase_conv(pa_ref, k, cw_ref, cb_ref, k * LANES)):
            a_s[k, pl.ds(r, NPH, stride=SUBLANES), :] = ur
    u = jnp.concatenate([a_s[k] for k in range(ntile)], axis=1)
    ub = _bf(u)
    half = D_BRANCH // 2
    lgs = []
    for hf in range(2):
        lgs.append(_dot(ub[:, hf * half:(hf + 1) * half], wg_ref[hf]))
        tick()
    coef = -LRU_C * _softplus(-ll_ref[...])
    for c in range(TILE // EW_ROWS):
        rows = slice(c * EW_ROWS, (c + 1) * EW_ROWS)
        for hf in range(2):
            lg = lgs[hf][rows] + bg_ref[:, hf * D_BRANCH:(hf + 1) * D_BRANCH]
            r = _sigmoid(lg[:, 0:half])
            i = _sigmoid(lg[:, half:2 * half])
            a = jnp.exp(coef[:, hf * half:(hf + 1) * half] * r)
            om = 1.0 - a * a
            mi = jnp.where(om > 0.0, om * lax.rsqrt(om), 0.0) * i
            for kk in range(half // LANES):
                k = hf * (half // LANES) + kk
                sl = slice(kk * LANES, (kk + 1) * LANES)
                b_s[k, rows, :] = mi[:, sl] * a_s[k, rows, :]
                a_s[k, rows, :] = a[:, sl]
    _lru_scan(a_s, b_s, p2_s, l2_s, c2_s, carry_ref, tick)
    for c in range(TILE // EW_ROWS):
        rows = slice(c * EW_ROWS, (c + 1) * EW_ROWS)
        zrows = slice(SUBLANES + c * EW_ROWS, SUBLANES + (c + 1) * EW_ROWS)
        for k in range(ntile):
            ycat_ref[rows, k * LANES:(k + 1) * LANES] = _bf(b_s[k, rows, :] * _silu(pa_ref[ntile + k, zrows, :]))
    _keep_tail(pa_ref)


def _shift_rows(c, k, row):
    n = c.shape[0]
    if k % SUBLANES == 0:
        return jnp.concatenate([jnp.zeros((k, c.shape[1]), c.dtype), c[:n - k]], axis=0)
    return jnp.where(row >= k, pltpu.roll(c, k, 0), 0.0)


def _mixer_s5(pb_ref, ycat_ref, u_s, y_s, carry_ref, toep_ref, eb_ref, f_ref, pw_ref, d_ref, gw_ref, gb_ref, tick):
    for j in range(S5_SLABS):
        u_s[j] = pb_ref[:, j * LANES:(j + 1) * LANES]
    J = S5_J
    row = _row_iota((J, S5_SW))
    span = S5_PH * S5_R
    for j in range(S5_SLABS):
        lhs = jnp.concatenate(
            [jnp.concatenate([u_s[j, pl.ds(S5_R * ph + t, J, stride=span), :] for t in range(S5_R)], axis=1)
             for ph in range(S5_PH)], axis=0)
        lhs = _bf(lhs)
        y_loc = _dot(lhs, toep_ref[j])
        e = _dot(lhs, eb_ref[j])
        tick()

        def pw(i):
            return pw_ref[j, 0, i:i + 1, :], pw_ref[j, 1, i:i + 1, :]

        l1r, l1i = pw(PW_L1)
        lr_ = [e[0:J, 0:S5_SW]]
        li_ = [e[0:J, S5_SW:2 * S5_SW]]
        for ph in range(1, S5_PH):
            mr, mi = _cmul(lr_[-1], li_[-1], l1r, l1i)
            lr_.append(mr + e[ph * J:(ph + 1) * J, 0:S5_SW])
            li_.append(mi + e[ph * J:(ph + 1) * J, S5_SW:2 * S5_SW])
        cr, ci = lr_[-1], li_[-1]
        for idx, k in enumerate(S5_HS_STEPS):
            qr, qi = pw(PW_HS + idx)
            mr, mi = _cmul(_shift_rows(cr, k, row), _shift_rows(ci, k, row), qr, qi)
            cr, ci = cr + mr, ci + mi
        cin_r = carry_ref[j, 0:1, :]
        cin_i = carry_ref[j, 1:2, :]
        tr = pw_ref[j, 0, PW_CARRY:PW_CARRY + J, :]
        ti = pw_ref[j, 1, PW_CARRY:PW_CARRY + J, :]
        mr, mi = _cmul(tr, ti, cin_r, cin_i)
        cr, ci = cr + mr, ci + mi
        carry_ref[j, 0:1, :] = cr[J - 1:J, :]
        carry_ref[j, 1:2, :] = ci[J - 1:J, :]
        cpr = jnp.where(row >= 1, pltpu.roll(cr, 1, 0), cin_r)
        cpi = jnp.where(row >= 1, pltpu.roll(ci, 1, 0), cin_i)
        sp_r, sp_i = [cpr], [cpi]
        for ph in range(1, S5_PH):
            qr, qi = pw(PW_L1 + ph - 1)
            mr, mi = _cmul(cpr, cpi, qr, qi)
            sp_r.append(lr_[ph - 1] + mr)
            sp_i.append(li_[ph - 1] + mi)
        sprev = jnp.concatenate([jnp.concatenate(sp_r, axis=0), jnp.concatenate(sp_i, axis=0)], axis=1)
        y = y_loc + _dot(_bf(sprev), f_ref[j])
        for ph in range(S5_PH):
            for t in range(S5_R):
                y_s[j, pl.ds(S5_R * ph + t, J, stride=span), :] = y[ph * J:(ph + 1) * J, t * LANES:(t + 1) * LANES]
    tick(2)
    for c in range(TILE // EW_ROWS):
        rows = slice(c * EW_ROWS, (c + 1) * EW_ROWS)
        for j in range(S5_SLABS):
            cols = slice(j * LANES, (j + 1) * LANES)
            y_s[j, rows, :] = _gelu_tanh(y_s[j, rows, :] + d_ref[:, cols] * pb_ref[rows, cols])
    glu = _dot(_bf(jnp.concatenate([y_s[j] for j in range(S5_SLABS)], axis=1)), gw_ref[...])
    tick(2)
    for c in range(TILE // EW_ROWS):
        rows = slice(c * EW_ROWS, (c + 1) * EW_ROWS)
        gate = _sigmoid(glu[rows] + gb_ref[...])
        for j in range(S5_SLABS):
            cols = slice(j * LANES, (j + 1) * LANES)
            zc = pb_ref[rows, D_BRANCH + j * LANES:D_BRANCH + (j + 1) * LANES]
            ycat_ref[rows, D_BRANCH + j * LANES:D_BRANCH + (j + 1) * LANES] = _bf(
                y_s[j, rows, :] * gate[:, cols] * _silu(zc))


def _mixer_gla(pc_ref, ycat_ref, qd_s, ki_s, ke_s, gl_s, o_s, st_s, wg_ref, bg_ref, nw_ref, tick):
    z_off = 2 * GLA_DK + D_BRANCH
    glow = pc_ref[:, 2 * GLA_DK + 2 * D_BRANCH:SEG_C_W]
    logits = _dot(_bf(glow), wg_ref[...])
    tick()
    tril = _tril_ones()
    for c in range(NCHUNK):
        rows = slice(c * CHUNK, (c + 1) * CHUNK)
        g = -_softplus(-(logits[rows] + bg_ref[...])) / GLA_TAU
        gc = _chunk_cumsum(tril, g)
        gl = gc[CHUNK - 1:CHUNK, :]
        k = pc_ref[rows, GLA_DK:2 * GLA_DK]
        qd_s[rows, :] = _bf(pc_ref[rows, 0:GLA_DK] * (GLA_HK ** -0.5) * jnp.exp(gc))
        ki_s[rows, :] = _bf(k * jnp.exp(-gc))
        ke_s[rows, :] = _bf(k * jnp.exp(gl - gc))
        gl_s[c * SUBLANES:(c + 1) * SUBLANES, :] = jnp.broadcast_to(jnp.exp(gl), (SUBLANES, GLA_DK))
        if c % 4 == 1:
            tick()

    lane = _lane_iota((1, LANES))
    m_lo = (lane < GLA_HK).astype(BF16)
    m_hi = (lane >= GLA_HK).astype(BF16)
    tri = (_lane_iota((CHUNK, CHUNK)) <= _row_iota((CHUNK, CHUNK)))

    for c in range(NCHUNK):
        if c % 2 == 0:
            tick()
        r0 = c * CHUNK
        rows = pl.ds(r0, CHUNK)
        for tl in range(GLA_DK // LANES):
            cols = slice(tl * LANES, (tl + 1) * LANES)
            qd = qd_s[rows, cols]
            ki = ki_s[rows, cols]
            ke = ke_s[rows, cols]
            dec = gl_s[pl.ds(c * SUBLANES, 1), cols]
            qstack = jnp.concatenate([qd * m_lo, qd * m_hi], axis=0)
            s2 = _dot_nt(qstack, ki)
            for hh in range(2):
                h = 2 * tl + hh
                vh = _bf(pc_ref[rows, 2 * GLA_DK + h * GLA_HV:2 * GLA_DK + (h + 1) * GLA_HV])
                sc = jnp.where(tri, s2[hh * CHUNK:(hh + 1) * CHUNK, :], 0.0)
                qh = qstack[hh * CHUNK:(hh + 1) * CHUNK, :]
                st = st_s[h]
                o = _dot(_bf(sc), vh) + _dot_nt(qh, _bf(st))
                o_s[h, rows, :] = o
                st_s[h] = st * dec + _dot_tn(vh, ke)

    tick()
    for c in range(TILE // EW_ROWS):
        rows = slice(c * EW_ROWS, (c + 1) * EW_ROWS)
        for h in range(GLA_HEADS):
            o = o_s[h, rows, :]
            o = o * lax.rsqrt(jnp.mean(o * o, axis=-1, keepdims=True) + EPS) * nw_ref[...]
            zh = pc_ref[rows, z_off + h * GLA_HV:z_off + (h + 1) * GLA_HV]
            ycat_ref[rows, 2 * D_BRANCH + h * GLA_HV:2 * D_BRANCH + (h + 1) * GLA_HV] = _bf(o * _silu(zh))


def _mixer_ssd(pd_ref, ycat_ref, xbc_s, ac_s, dt_s, y_s, st_s,
               cw_ref, cb_ref, dtb_ref, alog_ref, dsk_ref, nw_ref, tick):
    zt = D_BRANCH // LANES
    xt = 2 * D_BRANCH // LANES
    tick(2)
    for k in range(xt):
        for r, ur in enumerate(_phase_conv(pd_ref, zt + k, cw_ref, cb_ref, k * LANES)):
            xbc_s[k, pl.ds(r, NPH, stride=SUBLANES), :] = _silu(ur)
    a = -jnp.exp(alog_ref[...])
    tril = _tril_ones()
    for c in range(NCHUNK):
        rows = slice(c * CHUNK, (c + 1) * CHUNK)
        dt = _softplus(pd_ref[zt + xt, SUBLANES + c * CHUNK:SUBLANES + (c + 1) * CHUNK, :] + dtb_ref[...])
        dt_s[rows, :] = dt
        ac_s[rows, :] = _chunk_cumsum(tril, dt * a)
    tick()

    lane = _lane_iota((1, LANES))
    lo = lane < SSD_HDIM
    m_lo = lo.astype(F32)
    m_hi = 1.0 - m_lo
    lane_j = jnp.where(lo, lane, lane - SSD_HDIM)
    tri2 = lane_j <= _row_iota((CHUNK, LANES))
    npairs = SSD_HEADS // 2
    ppg = npairs // SSD_GROUPS

    for c in range(NCHUNK):
        tick()
        r0 = c * CHUNK
        rows = pl.ds(r0, CHUNK)
        ac = ac_s[rows, :]
        dtc = dt_s[rows, :]
        a_last = ac_s[pl.ds(r0 + CHUNK - 1, 1), :]
        dend = jnp.exp(a_last - ac) * dtc
        cdec = jnp.exp(a_last)
        act = jnp.concatenate([ac, ac], axis=0).T
        dtt = jnp.concatenate([dtc, dtc], axis=0).T
        for g in range(SSD_GROUPS):
            bm = xbc_s[npairs + g, rows, :]
            cm = xbc_s[npairs + SSD_GROUPS + g, rows, :]
            bmb = _bf(bm)
            cb2 = _dot_nt(_bf(cm), jnp.concatenate([bmb, bmb], axis=0))
            st = st_s[g]
            y_off = _dot(_bf(cm), _bf(st))
            xd, cdrow = [], []
            for pp in range(ppg):
                pr = g * ppg + pp
                h0, h1 = 2 * pr, 2 * pr + 1
                xp = xbc_s[pr, rows, :]
                acol = jnp.where(lo, jnp.broadcast_to(ac[:, h0:h0 + 1], (CHUNK, LANES)),
                                 jnp.broadcast_to(ac[:, h1:h1 + 1], (CHUNK, LANES)))
                arow = jnp.where(lo, act[h0:h0 + 1, :], act[h1:h1 + 1, :])
                dtrow = jnp.where(lo, dtt[h0:h0 + 1, :], dtt[h1:h1 + 1, :])
                seg = jnp.where(tri2, jnp.exp(jnp.where(tri2, acol - arow, 0.0)), 0.0)
                m = cb2 * seg * dtrow
                xbd = jnp.concatenate([xp * m_lo, xp * m_hi], axis=0)
                y = _dot(_bf(m), _bf(xbd))
                y = y + y_off[:, pp * LANES:(pp + 1) * LANES] * jnp.exp(acol)
                y = y + dsk_ref[:, pr * LANES:(pr + 1) * LANES] * xp
                y_s[pr, rows, :] = y
                dcol = jnp.where(lo, jnp.broadcast_to(dend[:, h0:h0 + 1], (CHUNK, LANES)),
                                 jnp.broadcast_to(dend[:, h1:h1 + 1], (CHUNK, LANES)))
                xd.append(xp * dcol)
                cdrow.append(jnp.where(lo, jnp.broadcast_to(cdec[:, h0:h0 + 1], (1, LANES)),
                                       jnp.broadcast_to(cdec[:, h1:h1 + 1], (1, LANES))))
            snew = _dot_tn(bmb, _bf(jnp.concatenate(xd, axis=1)))
            st_s[g] = st * jnp.concatenate(cdrow, axis=1) + snew

    for c in range(TILE // EW_ROWS):
        rows = slice(c * EW_ROWS, (c + 1) * EW_ROWS)
        zrows = slice(SUBLANES + c * EW_ROWS, SUBLANES + (c + 1) * EW_ROWS)
        y = jnp.concatenate([y_s[pr, rows, :] * _silu(pd_ref[pr, zrows, :]) for pr in range(npairs)], axis=1)
        y = y * lax.rsqrt(jnp.mean(y * y, axis=-1, keepdims=True) + EPS) * nw_ref[...]
        ycat_ref[rows, 3 * D_BRANCH:4 * D_BRANCH] = _bf(y)
    _keep_tail(pd_ref)


class _Interleave:
    def __init__(self):
        self.queue = []

    def load(self, thunks):
        self.flush()
        self.queue = list(thunks)

    def tick(self, n=1):
        for _ in range(n):
            if self.queue:
                self.queue.pop(0)()

    def flush(self):
        while self.queue:
            self.queue.pop(0)()


PROJ_CHUNK = 256


VEC_ROWS = 16
(VR_SSD_CW, VR_NORM, VR_LRU_BG, VR_SSD_CB, VR_NORM_F, VR_LRU_CW, VR_SSD_DN, VR_MISC) = (0, 4, 5, 6, 7, 8, 12, 13)


class _VecRows:
    def __init__(self, ref, row, nrows, c0, c1):
        self.ref, self.row, self.nrows, self.c0, self.c1 = ref, row, nrows, c0, c1

    def __getitem__(self, idx):
        rows, cols = (slice(None), slice(None)) if idx is Ellipsis else idx
        r0, r1, _ = rows.indices(self.nrows)
        a, b, _ = cols.indices(self.c1 - self.c0)
        return self.ref[self.row + r0:self.row + r1, self.c0 + a:self.c0 + b]


def _pack_vectors(norm_w, a_bg, lru_conv_w, lru_conv_b, lru_l, s5_d, s5_glu_b, gla_b_gate, gla_norm_w,
                  ssd_conv_w, ssd_conv_b, dtb, alog, dsk, ssd_norm_w, norm_f_w):
    nl = norm_w.shape[0]
    r1 = lambda a: a.reshape(nl, 1, -1).astype(F32)
    halves = jnp.stack([lru_conv_b, lru_l, s5_d, s5_glu_b], axis=1).astype(F32)
    misc = jnp.concatenate([gla_b_gate, gla_norm_w, dtb, alog,
                            jnp.zeros((nl, D_MODEL - GLA_DK - 3 * LANES), F32)], axis=-1)
    rows = [ssd_conv_w.astype(F32), r1(norm_w), r1(a_bg), r1(ssd_conv_b),
            jnp.broadcast_to(norm_f_w.reshape(1, 1, D_MODEL).astype(F32), (nl, 1, D_MODEL)),
            jnp.concatenate([lru_conv_w.astype(F32), halves], axis=-1),
            jnp.concatenate([r1(dsk), r1(ssd_norm_w)], axis=-1), r1(misc)]
    packed = jnp.concatenate(rows, axis=1)
    return jnp.concatenate([packed, jnp.zeros((nl, VEC_ROWS - packed.shape[1], D_MODEL), F32)], axis=1)


def _layer_kernel(final, tiles_per_seq,
                  x_ref, vec_ref, win_ref, wind_ref, a_wg,
                  b_toep, b_eb, b_f, b_pw, b_gw, c_wg, wout_ref,
                  o_ref,
                  pa_s, pb_s, pc_s, pd_s, xs_s, hb_s,
                  ycat, buf0, buf1, p2_s, l2_s, c2_s, lru_carry, s5_carry,
                  qd_s, ki_s, ke_s, gl_s, gla_st, xbc_s, ac_s, dt_s, ssd_st):
    def vec(row, c0, c1, nrows=1):
        return _VecRows(vec_ref, row, nrows, c0, c1)

    hw = D_BRANCH
    nw_ref, a_bg, d_cb, nf_ref = (vec(r, 0, D_MODEL) for r in (VR_NORM, VR_LRU_BG, VR_SSD_CB, VR_NORM_F))
    d_cw = vec(VR_SSD_CW, 0, D_MODEL, CONV_W)
    a_cw = vec(VR_LRU_CW, 0, hw, CONV_W)
    a_cb, a_l, b_d, b_gb = (vec(VR_LRU_CW + i, hw, 2 * hw) for i in range(4))
    d_dsk, d_nw = vec(VR_SSD_DN, 0, hw), vec(VR_SSD_DN, hw, 2 * hw)
    c_bg = vec(VR_MISC, 0, GLA_DK)
    c_nw, d_dtb, d_alog = (vec(VR_MISC, GLA_DK + i * LANES, GLA_DK + (i + 1) * LANES) for i in range(3))
    g = pl.program_id(0)

    @pl.when(g == 0)
    def _():
        pa_s[...] = jnp.zeros_like(pa_s)
        pb_s[...] = jnp.zeros_like(pb_s)
        pc_s[...] = jnp.zeros_like(pc_s)
        pd_s[...] = jnp.zeros_like(pd_s)
        xs_s[...] = jnp.zeros_like(xs_s)
        hb_s[...] = jnp.zeros_like(hb_s)

    @pl.when((g == 0) | (lax.rem(g - 1, tiles_per_seq) == 0))
    def _():
        pa_s[:, 0:SUBLANES, :] = jnp.zeros((pa_s.shape[0], SUBLANES, LANES), F32)
        pd_s[:, 0:SUBLANES, :] = jnp.zeros((pd_s.shape[0], SUBLANES, LANES), F32)
        lru_carry[...] = jnp.zeros_like(lru_carry)
        s5_carry[...] = jnp.zeros_like(s5_carry)
        gla_st[...] = jnp.zeros_like(gla_st)
        ssd_st[...] = jnp.zeros_like(ssd_st)

    def proj_chunks(w_ref, w_off, dst_ref, width, tiled):
        out = []
        for c0 in range(0, width, PROJ_CHUNK):
            c1 = min(c0 + PROJ_CHUNK, width)

            def thunk(c0=c0, c1=c1):
                res = _dot(hb_s[...], w_ref[:, w_off + c0:w_off + c1])
                if tiled:
                    for i in range((c1 - c0) // LANES):
                        dst_ref[c0 // LANES + i, SUBLANES:SUBLANES + TILE, :] = res[:, i * LANES:(i + 1) * LANES]
                else:
                    dst_ref[:, c0:c1] = res
            out.append(thunk)
        return out

    def out_chunks(m):
        out = []
        rows = slice(m * D_BRANCH, (m + 1) * D_BRANCH)
        for c0 in range(0, D_MODEL, PROJ_CHUNK):
            cols = slice(c0, c0 + PROJ_CHUNK)

            def thunk(cols=cols):
                base = xs_s[:, cols] if m == 0 else o_ref[:, cols]
                o_ref[:, cols] = base + _dot(ycat[:, rows], wout_ref[rows, cols])
            out.append(thunk)
        return out

    il = _Interleave()
    il.load(proj_chunks(wind_ref, 0, pd_s, SEG_D_W, True))
    _mixer_lru(pa_s, ycat, buf0, buf1, p2_s, l2_s, c2_s, lru_carry, a_cw, a_cb, a_wg, a_bg, a_l, il.tick)
    il.flush()
    for c in range(TILE // EW_ROWS):
        rows = slice(c * EW_ROWS, (c + 1) * EW_ROWS)
        hb_s[rows, :] = _bf(_rmsnorm(x_ref[rows, :], nw_ref[...]))
    il.load(proj_chunks(win_ref, SEG_A, pa_s, SEG_B - SEG_A, True) + out_chunks(0))
    _mixer_s5(pb_s, ycat, buf0, buf1, s5_carry, b_toep, b_eb, b_f, b_pw, b_d, b_gw, b_gb, il.tick)
    il.load(proj_chunks(win_ref, SEG_B, pb_s, SEG_C - SEG_B, False) + out_chunks(1))
    _mixer_gla(pc_s, ycat, qd_s, ki_s, ke_s, gl_s, buf1, gla_st, c_wg, c_bg, c_nw, il.tick)
    il.load(proj_chunks(win_ref, SEG_C, pc_s, SEG_C_W, False) + out_chunks(2))
    _mixer_ssd(pd_s, ycat, xbc_s, ac_s, dt_s, buf1, ssd_st, d_cw, d_cb, d_dtb, d_alog, d_dsk, d_nw, il.tick)
    il.load(out_chunks(3))
    il.flush()
    if final:
        for c in range(TILE // EW_ROWS):
            rows = slice(c * EW_ROWS, (c + 1) * EW_ROWS)
            o_ref[rows, :] = _rmsnorm(o_ref[rows, :], nf_ref[...])
    xs_s[...] = x_ref[...]


def _layer_call(layer, final, x, consts):
    bsz, seq, _ = x.shape
    assert seq % TILE == 0
    tiles_per_seq = seq // TILE
    ntiles = bsz * tiles_per_seq

    def wspec(arr):
        nd = arr.ndim - 1
        return pl.BlockSpec((None,) + arr.shape[1:], lambda g, _n=nd: (layer,) + (0,) * _n,
                            pipeline_mode=pl.Buffered(1))

    def cspec(arr):
        nd = arr.ndim
        return pl.BlockSpec(arr.shape, lambda g, _n=nd: (0,) * _n, pipeline_mode=pl.Buffered(1))

    def tile_index(t):
        return (t // tiles_per_seq, t % tiles_per_seq, 0)

    per_layer = consts["per_layer"]
    shared = consts["shared"]
    in_x = pl.BlockSpec((None, TILE, D_MODEL), lambda g: tile_index(jnp.minimum(g, ntiles - 1)))
    out_x = pl.BlockSpec((None, TILE, D_MODEL), lambda g: tile_index(jnp.maximum(g - 1, 0)))
    scratch = [
        pltpu.VMEM(((SEG_B - SEG_A) // LANES, SUBLANES + TILE, LANES), F32),
        pltpu.VMEM((TILE, SEG_C - SEG_B), F32),
        pltpu.VMEM((TILE, SEG_C_W), F32),
        pltpu.VMEM((SEG_D_W // LANES, SUBLANES + TILE, LANES), F32),
        pltpu.VMEM((TILE, D_MODEL), F32),
        pltpu.VMEM((TILE, D_MODEL), BF16),
        pltpu.VMEM((TILE, 4 * D_BRANCH), BF16),
        pltpu.VMEM((4, TILE, LANES), F32),
        pltpu.VMEM((4, TILE, LANES), F32),
        pltpu.VMEM((4, TILE // 8, LANES), F32),
        pltpu.VMEM((4, TILE // 8, LANES), F32),
        pltpu.VMEM((4, TILE // 8, LANES), F32),
        pltpu.VMEM((SUBLANES, D_BRANCH), F32),
        pltpu.VMEM((S5_SLABS, SUBLANES, S5_SW), F32),
        pltpu.VMEM((TILE, GLA_DK), BF16),
        pltpu.VMEM((TILE, GLA_DK), BF16),
        pltpu.VMEM((TILE, GLA_DK), BF16),
        pltpu.VMEM((NCHUNK * SUBLANES, GLA_DK), F32),
        pltpu.VMEM((GLA_HEADS, GLA_HV, LANES), F32),
        pltpu.VMEM((2 * D_BRANCH // LANES, TILE, LANES), F32),
        pltpu.VMEM((TILE, LANES), F32),
        pltpu.VMEM((TILE, LANES), F32),
        pltpu.VMEM((SSD_GROUPS, SSD_STATE, D_BRANCH // SSD_GROUPS), F32),
    ]
    return pl.pallas_call(
        functools.partial(_layer_kernel, final, tiles_per_seq),
        grid=(ntiles + 1,),
        in_specs=[in_x] + [wspec(a) for a in per_layer] + [cspec(a) for a in shared],
        out_specs=out_x,
        out_shape=jax.ShapeDtypeStruct(x.shape, F32),
        scratch_shapes=scratch,
        compiler_params=pltpu.CompilerParams(
            dimension_semantics=("arbitrary",),
            vmem_limit_bytes=VMEM_LIMIT_BYTES),
        name=f"layer{layer}",
    )(x, *per_layer, *shared)


def _prepare(norm_w, w_in, lru_conv_w, lru_conv_b, lru_w_r, lru_b_r, lru_w_i, lru_b_i, lru_l,
             s5_lam_re, s5_lam_im, s5_log_dt, s5_b_re, s5_b_im, s5_c_re, s5_c_im, s5_d, s5_glu_w, s5_glu_b,
             gla_w_gate, gla_b_gate, gla_norm_w,
             ssd_conv_w, ssd_conv_b, ssd_dt_bias, ssd_a_log, ssd_d, ssd_norm_w,
             w_out, norm_f_w):
    nl = w_in.shape[0]
    w_abc = jnp.pad(w_in[:, :, :ORIG_C_END].astype(BF16), ((0, 0), (0, 0), (0, SEG_D - ORIG_C_END)))
    w_d = jnp.pad(w_in[:, :, ORIG_C_END:].astype(BF16), ((0, 0), (0, 0), (0, SEG_D_W - (ORIG_D_IN - ORIG_C_END))))
    hph = LRU_HEADS // 2
    eye = jnp.eye(hph, dtype=F32)

    def bd(w):
        w = w.reshape(nl, 2, hph, LRU_HDIM, LRU_HDIM)
        return jnp.einsum("lfhij,hk->lfhikj", w, eye).reshape(nl, 2, hph * LRU_HDIM, hph * LRU_HDIM)

    a_wg = jnp.concatenate([bd(lru_w_r), bd(lru_w_i)], axis=-1).astype(BF16)
    half = D_BRANCH // 2
    br = lru_b_r.reshape(nl, 2, half)
    bi = lru_b_i.reshape(nl, 2, half)
    a_bg = jnp.concatenate([br, bi], axis=-1).reshape(nl, 2 * D_BRANCH)
    toep, ebm, fm, pw = _s5_prepare(s5_lam_re, s5_lam_im, s5_log_dt, s5_b_re, s5_b_im, s5_c_re, s5_c_im)
    c_wg = jnp.concatenate([gla_w_gate, jnp.zeros((nl, LANES - GLA_RANK, GLA_DK), gla_w_gate.dtype)], axis=1).astype(BF16)
    pad_h = lambda a: jnp.concatenate([a, jnp.zeros((nl, LANES - SSD_HEADS), a.dtype)], axis=-1)
    vecs = _pack_vectors(norm_w, a_bg, lru_conv_w, lru_conv_b, lru_l, s5_d, s5_glu_b, gla_b_gate, gla_norm_w,
                         ssd_conv_w, ssd_conv_b, pad_h(ssd_dt_bias), pad_h(ssd_a_log),
                         jnp.repeat(ssd_d, SSD_HDIM, axis=-1), ssd_norm_w, norm_f_w)
    per_layer = [vecs, w_abc, w_d, a_wg, toep, ebm, fm, pw, s5_glu_w.astype(BF16), c_wg, w_out.astype(BF16)]
    return {"per_layer": per_layer, "shared": []}


def kernel(x, norm_w, w_in, lru_conv_w, lru_conv_b, lru_w_r, lru_b_r, lru_w_i, lru_b_i, lru_l, s5_lam_re, s5_lam_im, s5_log_dt, s5_b_re, s5_b_im, s5_c_re, s5_c_im, s5_d, s5_glu_w, s5_glu_b, gla_w_gate, gla_b_gate, gla_norm_w, ssd_conv_w, ssd_conv_b, ssd_dt_bias, ssd_a_log, ssd_d, ssd_norm_w, w_out, norm_f_w):
    consts = _prepare(norm_w, w_in, lru_conv_w, lru_conv_b, lru_w_r, lru_b_r, lru_w_i, lru_b_i, lru_l,
                      s5_lam_re, s5_lam_im, s5_log_dt, s5_b_re, s5_b_im, s5_c_re, s5_c_im, s5_d, s5_glu_w, s5_glu_b,
                      gla_w_gate, gla_b_gate, gla_norm_w,
                      ssd_conv_w, ssd_conv_b, ssd_dt_bias, ssd_a_log, ssd_d, ssd_norm_w,
                      w_out, norm_f_w)
    nl = w_in.shape[0]
    for layer in range(nl):
        x = _layer_call(layer, layer == nl - 1, x, consts)
    return x
```

```python
import functools
import math

import jax
import jax.numpy as jnp
from jax import lax
from jax.experimental import pallas as pl
from jax.experimental.pallas import tpu as pltpu

F32 = jnp.float32
BF16 = jnp.bfloat16

D_MODEL = 1024
D_BRANCH = 512
CONV_W = 4
EPS = 1e-6
LRU_HEADS = 8
LRU_HDIM = 64
LRU_C = 8.0
S5_GROUP = 16
S5_GROUPS = 32
S5_STATE = 64
GLA_HEADS = 4
GLA_DK = 256
GLA_HK = 64
GLA_HV = 128
GLA_RANK = 16
GLA_TAU = 16.0
SSD_HEADS = 8
SSD_HDIM = 64
SSD_GROUPS = 2
SSD_STATE = 128
CHUNK = 64

LANES = 128
SUBLANES = 8
VMEM_LIMIT_BYTES = 62 * 1024 * 1024

TILE = 512
NCHUNK = TILE // CHUNK
S5_R = 4
S5_PH = 8
S5_NB = TILE // S5_R
S5_J = S5_NB // S5_PH
S5_SLABS = D_BRANCH // LANES
S5_GPS = LANES // S5_GROUP
S5_SW = S5_GPS * S5_STATE
S5_HS_STEPS = tuple(k for k in (1, 2, 4, 8, 16, 32) if k < S5_J)
PW_L1 = 0
PW_HS = S5_PH - 1
PW_CARRY = PW_HS + len(S5_HS_STEPS)
PW_ROWS = PW_CARRY + S5_J

SEG_A = 0
SEG_B = 1024
SEG_C = 2048
SEG_C_W = 1664
SEG_D = SEG_C + SEG_C_W
SEG_D_W = 1664
N_IN = SEG_D + SEG_D_W
ORIG_C_END = 3600
ORIG_D_IN = 5144


def _dot(a, b):
    return jnp.dot(a, b, preferred_element_type=F32)


def _dot_nt(a, b):
    return lax.dot_general(a, b, (((1,), (1,)), ((), ())), preferred_element_type=F32)


def _dot_tn(a, b):
    return lax.dot_general(a, b, (((0,), (0,)), ((), ())), preferred_element_type=F32)


def _bf(x):
    return x.astype(BF16)


def _softplus(x):
    return jnp.maximum(x, 0.0) + jnp.log1p(jnp.exp(-jnp.abs(x)))


def _sigmoid(x):
    return 0.5 * jnp.tanh(0.5 * x) + 0.5


def _silu(x):
    return x * _sigmoid(x)


def _gelu_tanh(x):
    c = math.sqrt(2.0 / math.pi)
    return 0.5 * x * (1.0 + jnp.tanh(c * (x + 0.044715 * (x * x * x))))


def _cmul(ar, ai, br, bi):
    return ar * br - ai * bi, ar * bi + ai * br


def _split_hi_lo(x):
    hi = x.astype(BF16)
    lo = (x - hi.astype(F32)).astype(BF16)
    return hi, lo


def _chunk_cumsum(x):
    hi, lo = _split_hi_lo(x)
    lo2 = (x - hi.astype(F32) - lo.astype(F32)).astype(BF16)
    ri = lax.broadcasted_iota(jnp.int32, (CHUNK, CHUNK), 0)
    ci = lax.broadcasted_iota(jnp.int32, (CHUNK, CHUNK), 1)
    t = (ci <= ri).astype(BF16)
    out = []
    for c in range(x.shape[0] // CHUNK):
        rows = slice(c * CHUNK, (c + 1) * CHUNK)
        out.append(_dot(t, hi[rows]) + _dot(t, lo[rows]) + _dot(t, lo2[rows]))
    return jnp.concatenate(out, axis=0)


def _s5_prep_kernel(lr_ref, li_ref, ldt_ref, lrc_ref, lic_ref, bt_re_ref, bt_im_ref, c_re_ref, c_im_ref,
                    ct_re_ref, ct_im_ref, toep_ref, eb_ref, f_ref, pw_ref):
    def discretise(lr, li, dt):
        mag = jnp.exp(lr * dt)
        ab_re = mag * jnp.cos(li * dt)
        ab_im = mag * jnp.sin(li * dt)
        den = lr * lr + li * li
        nr = ab_re - 1.0
        coef_re = (nr * lr + ab_im * li) / den
        coef_im = (ab_im * lr - nr * li) / den
        return ab_re, ab_im, coef_re, coef_im

    def powers(ab_re, ab_im, n):
        out = [(jnp.ones_like(ab_re), jnp.zeros_like(ab_re)), (ab_re, ab_im)]
        for _ in range(2, n + 1):
            out.append(_cmul(out[-1][0], out[-1][1], ab_re, ab_im))
        return out

    ab_re, ab_im, cf_re, cf_im = discretise(lr_ref[...], li_ref[...], jnp.exp(ldt_ref[...]))
    low = powers(ab_re, ab_im, S5_R)
    l1 = low[S5_R]
    l1_pows = [l1]
    for _ in range(2, S5_PH + 1):
        l1_pows.append(_cmul(l1_pows[-1][0], l1_pows[-1][1], l1[0], l1[1]))
    l2 = l1_pows[S5_PH - 1]
    for r in range(1, S5_PH):
        pw_ref[0, PW_L1 + r - 1] = l1_pows[r - 1][0]
        pw_ref[1, PW_L1 + r - 1] = l1_pows[r - 1][1]
    sq = l2
    for i, k in enumerate(S5_HS_STEPS):
        pw_ref[0, PW_HS + i] = sq[0]
        pw_ref[1, PW_HS + i] = sq[1]
        sq = _cmul(sq[0], sq[1], sq[0], sq[1])
    cur = l2
    for j in range(S5_J):
        pw_ref[0, PW_CARRY + j] = cur[0]
        pw_ref[1, PW_CARRY + j] = cur[1]
        cur = _cmul(cur[0], cur[1], l2[0], l2[1])

    toep_ref[...] = jnp.zeros_like(toep_ref)
    eb_ref[...] = jnp.zeros_like(eb_ref)
    f_ref[...] = jnp.zeros_like(f_ref)
    hi = lax.Precision.HIGHEST
    for g in range(S5_GPS):
        row_pw = [(p[0][g:g + 1, :], p[1][g:g + 1, :]) for p in low]
        bbt_re, bbt_im = _cmul(cf_re[g:g + 1, :], cf_im[g:g + 1, :], bt_re_ref[g], bt_im_ref[g])
        for t in range(S5_R):
            e_re, e_im = _cmul(row_pw[S5_R - 1 - t][0], row_pw[S5_R - 1 - t][1], bbt_re, bbt_im)
            rows = slice(t * LANES + g * S5_GROUP, t * LANES + (g + 1) * S5_GROUP)
            eb_ref[rows, g * S5_STATE:(g + 1) * S5_STATE] = _bf(e_re)
            eb_ref[rows, S5_SW + g * S5_STATE:S5_SW + (g + 1) * S5_STATE] = _bf(e_im)
        for k in range(S5_R):
            cl_re, cl_im = _cmul(c_re_ref[g], c_im_ref[g], row_pw[k][0], row_pw[k][1])
            kt = (lax.dot_general(bbt_re, cl_re, (((1,), (1,)), ((), ())), preferred_element_type=F32, precision=hi)
                  - lax.dot_general(bbt_im, cl_im, (((1,), (1,)), ((), ())), preferred_element_type=F32, precision=hi))
            ktb = _bf(kt)
            for t in range(S5_R - k):
                r = t + k
                toep_ref[t * LANES + g * S5_GROUP:t * LANES + (g + 1) * S5_GROUP,
                         r * LANES + g * S5_GROUP:r * LANES + (g + 1) * S5_GROUP] = ktb
        abc_re, abc_im, _, _ = discretise(lrc_ref[g], lic_ref[g], jnp.exp(ldt_ref[g:g + 1, :]))
        col_pw = powers(abc_re, abc_im, S5_R)
        for r in range(S5_R):
            f_re, f_im = _cmul(ct_re_ref[g], ct_im_ref[g], col_pw[r + 1][0], col_pw[r + 1][1])
            cols = slice(r * LANES + g * S5_GROUP, r * LANES + (g + 1) * S5_GROUP)
            f_ref[g * S5_STATE:(g + 1) * S5_STATE, cols] = _bf(f_re)
            f_ref[S5_SW + g * S5_STATE:S5_SW + (g + 1) * S5_STATE, cols] = _bf(-f_im)


def _s5_prepare(lam_re, lam_im, log_dt, b_re, b_im, c_re, c_im):
    nl = lam_re.shape[0]
    g8 = S5_GPS
    rl = S5_R * LANES

    def im4(l, j):
        return (l, j, 0, 0)

    def im3(l, j):
        return (l, j, 0)

    gsn = pl.BlockSpec((None, g8, S5_GROUP, S5_STATE), im4)
    toep, ebm, fm, pw = pl.pallas_call(
        _s5_prep_kernel,
        grid=(nl, S5_SLABS),
        in_specs=[
            pl.BlockSpec((None, g8, S5_STATE), im3),
            pl.BlockSpec((None, g8, S5_STATE), im3),
            pl.BlockSpec((None, g8, 1), im3),
            pl.BlockSpec((None, g8, S5_STATE, 1), im4),
            pl.BlockSpec((None, g8, S5_STATE, 1), im4),
            gsn, gsn, gsn, gsn,
            pl.BlockSpec((None, g8, S5_STATE, S5_GROUP), im4),
            pl.BlockSpec((None, g8, S5_STATE, S5_GROUP), im4),
        ],
        out_specs=[
            pl.BlockSpec((None, None, rl, rl), im4),
            pl.BlockSpec((None, None, rl, 2 * S5_SW), im4),
            pl.BlockSpec((None, None, 2 * S5_SW, rl), im4),
            pl.BlockSpec((None, None, 2, PW_ROWS, g8, S5_STATE), lambda l, j: (l, j, 0, 0, 0, 0)),
        ],
        out_shape=[
            jax.ShapeDtypeStruct((nl, S5_SLABS, rl, rl), BF16),
            jax.ShapeDtypeStruct((nl, S5_SLABS, rl, 2 * S5_SW), BF16),
            jax.ShapeDtypeStruct((nl, S5_SLABS, 2 * S5_SW, rl), BF16),
            jax.ShapeDtypeStruct((nl, S5_SLABS, 2, PW_ROWS, g8, S5_STATE), F32),
        ],
        name="s5_prepare",
    )(lam_re, lam_im, log_dt[..., None], lam_re[..., None], lam_im[..., None],
      jnp.swapaxes(b_re, -1, -2), jnp.swapaxes(b_im, -1, -2), c_re, c_im,
      jnp.swapaxes(c_re, -1, -2), jnp.swapaxes(c_im, -1, -2))
    return toep, ebm, fm, pw.reshape(nl, S5_SLABS, 2, PW_ROWS, S5_SW)


def _rmsnorm(x, w):
    return x * lax.rsqrt(jnp.mean(x * x, axis=-1, keepdims=True) + EPS) * w


NPH = TILE // SUBLANES


def _phase_conv(p_ref, tile, w_ref, b_ref, c0):
    cols = slice(c0, c0 + LANES)
    w = w_ref[:, cols]
    b = b_ref[:, cols]
    xs = {r: p_ref[tile, pl.ds(SUBLANES + r, NPH, stride=SUBLANES), :] for r in range(1 - CONV_W, SUBLANES)}
    out = []
    for r in range(SUBLANES):
        acc = b + w[CONV_W - 1:CONV_W, :] * xs[r]
        for k in range(CONV_W - 1):
            acc = acc + w[k:k + 1, :] * xs[r - (CONV_W - 1) + k]
        out.append(acc)
    return out


def _keep_tail(p_ref):
    p_ref[:, 0:SUBLANES, :] = p_ref[:, TILE:TILE + SUBLANES, :]


def _row_iota(shape):
    return lax.broadcasted_iota(jnp.int32, shape, 0)


def _lane_iota(shape):
    return lax.broadcasted_iota(jnp.int32, shape, 1)


def _lru_scan(a_s, b_s, p2_s, l2_s, c2_s, carry_ref, tick):
    n1 = TILE // 8
    n2 = n1 // 8
    assert n2 == SUBLANES
    row = _row_iota((SUBLANES, LANES))
    for k in range(D_BRANCH // LANES):
        tick()

        def ph1(ref, r):
            return ref.at[k, pl.ds(r, n1, stride=8), :]

        def ph2(ref, r):
            return ref.at[k, pl.ds(r, n2, stride=8), :]

        p = ph1(a_s, 0)[...]
        l = ph1(b_s, 0)[...]
        for r in range(1, 8):
            ar = ph1(a_s, r)[...]
            br = ph1(b_s, r)[...]
            l = ar * l + br
            p = ar * p
            ph1(a_s, r)[...] = p
            ph1(b_s, r)[...] = l
        p2_s[k] = p
        l2_s[k] = l
        p2 = [ph2(p2_s, 0)[...]]
        l2 = [ph2(l2_s, 0)[...]]
        for r in range(1, 8):
            ar = ph2(p2_s, r)[...]
            br = ph2(l2_s, r)[...]
            l2.append(ar * l2[-1] + br)
            p2.append(ar * p2[-1])
        p3, l3 = p2[7], l2[7]
        for sh in (1, 2, 4):
            psh = jnp.where(row >= sh, pltpu.roll(p3, sh, 0), 1.0)
            lsh = jnp.where(row >= sh, pltpu.roll(l3, sh, 0), 0.0)
            l3 = p3 * lsh + l3
            p3 = p3 * psh
        cin = carry_ref[:, k * LANES:(k + 1) * LANES]
        c3 = l3 + p3 * cin
        c3prev = jnp.where(row >= 1, pltpu.roll(c3, 1, 0), cin)
        carry_ref[:, k * LANES:(k + 1) * LANES] = jnp.broadcast_to(c3[SUBLANES - 1:SUBLANES, :], (SUBLANES, LANES))
        ph2(c2_s, 0)[...] = c3prev
        for r in range(1, 8):
            ph2(c2_s, r)[...] = l2[r - 1] + p2[r - 1] * c3prev
        cprev = c2_s[k]
        for r in range(8):
            ph1(b_s, r)[...] = ph1(b_s, r)[...] + ph1(a_s, r)[...] * cprev


def _mixer_lru(pa_ref, ycat_ref, a_s, b_s, p2_s, l2_s, c2_s, carry_ref,
               cw_ref, cb_ref, wg_ref, bg_ref, ll_ref, tick):
    tick()
    ntile = D_BRANCH // LANES
    for k in range(ntile):
        for r, ur in enumerate(_phase_conv(pa_ref, k, cw_ref, cb_ref, k * LANES)):
            a_s[k, pl.ds(r, NPH, stride=SUBLANES), :] = ur
    u = jnp.concatenate([a_s[k] for k in range(ntile)], axis=1)
    ub = _bf(u)
    half = D_BRANCH // 2
    rs, is_ = [], []
    for hf in range(2):
        lg = _dot(ub[:, hf * half:(hf + 1) * half], wg_ref[hf]) + bg_ref[:, hf * D_BRANCH:(hf + 1) * D_BRANCH]
        tick()
        rs.append(_sigmoid(lg[:, 0:half]))
        is_.append(_sigmoid(lg[:, half:2 * half]))
    r = jnp.concatenate(rs, axis=1)
    i = jnp.concatenate(is_, axis=1)
    coef = -LRU_C * _softplus(-ll_ref[...])
    a = jnp.exp(coef * r)
    om = 1.0 - a * a
    mult = jnp.where(om > 0.0, om * lax.rsqrt(om), 0.0)
    b = mult * i * u
    for k in range(D_BRANCH // LANES):
        a_s[k] = a[:, k * LANES:(k + 1) * LANES]
        b_s[k] = b[:, k * LANES:(k + 1) * LANES]
    _lru_scan(a_s, b_s, p2_s, l2_s, c2_s, carry_ref, tick)
    for k in range(D_BRANCH // LANES):
        ycat_ref[:, k * LANES:(k + 1) * LANES] = _bf(
            b_s[k] * _silu(pa_ref[ntile + k, SUBLANES:SUBLANES + TILE, :]))
    _keep_tail(pa_ref)


def _shift_rows(c, k, row):
    n = c.shape[0]
    if k % SUBLANES == 0:
        return jnp.concatenate([jnp.zeros((k, c.shape[1]), c.dtype), c[:n - k]], axis=0)
    return jnp.where(row >= k, pltpu.roll(c, k, 0), 0.0)


def _mixer_s5(pb_ref, ycat_ref, u_s, y_s, carry_ref, toep_ref, eb_ref, f_ref, pw_ref, d_ref, gw_ref, gb_ref, tick):
    for j in range(S5_SLABS):
        u_s[j] = pb_ref[:, j * LANES:(j + 1) * LANES]
    J = S5_J
    row = _row_iota((J, S5_SW))
    span = S5_PH * S5_R
    for j in range(S5_SLABS):
        lhs = jnp.concatenate(
            [jnp.concatenate([u_s[j, pl.ds(S5_R * ph + t, J, stride=span), :] for t in range(S5_R)], axis=1)
             for ph in range(S5_PH)], axis=0)
        lhs = _bf(lhs)
        y_loc = _dot(lhs, toep_ref[j])
        e = _dot(lhs, eb_ref[j])
        tick()

        def pw(i):
            return pw_ref[j, 0, i:i + 1, :], pw_ref[j, 1, i:i + 1, :]

        l1r, l1i = pw(PW_L1)
        lr_ = [e[0:J, 0:S5_SW]]
        li_ = [e[0:J, S5_SW:2 * S5_SW]]
        for ph in range(1, S5_PH):
            mr, mi = _cmul(lr_[-1], li_[-1], l1r, l1i)
            lr_.append(mr + e[ph * J:(ph + 1) * J, 0:S5_SW])
            li_.append(mi + e[ph * J:(ph + 1) * J, S5_SW:2 * S5_SW])
        cr, ci = lr_[-1], li_[-1]
        for idx, k in enumerate(S5_HS_STEPS):
            qr, qi = pw(PW_HS + idx)
            mr, mi = _cmul(_shift_rows(cr, k, row), _shift_rows(ci, k, row), qr, qi)
            cr, ci = cr + mr, ci + mi
        cin_r = carry_ref[j, 0:1, :]
        cin_i = carry_ref[j, 1:2, :]
        tr = pw_ref[j, 0, PW_CARRY:PW_CARRY + J, :]
        ti = pw_ref[j, 1, PW_CARRY:PW_CARRY + J, :]
        mr, mi = _cmul(tr, ti, cin_r, cin_i)
        cr, ci = cr + mr, ci + mi
        carry_ref[j, 0:1, :] = cr[J - 1:J, :]
        carry_ref[j, 1:2, :] = ci[J - 1:J, :]
        cpr = jnp.where(row >= 1, pltpu.roll(cr, 1, 0), cin_r)
        cpi = jnp.where(row >= 1, pltpu.roll(ci, 1, 0), cin_i)
        sp_r, sp_i = [cpr], [cpi]
        for ph in range(1, S5_PH):
            qr, qi = pw(PW_L1 + ph - 1)
            mr, mi = _cmul(cpr, cpi, qr, qi)
            sp_r.append(lr_[ph - 1] + mr)
            sp_i.append(li_[ph - 1] + mi)
        sprev = jnp.concatenate([jnp.concatenate(sp_r, axis=0), jnp.concatenate(sp_i, axis=0)], axis=1)
        y = y_loc + _dot(_bf(sprev), f_ref[j])
        for ph in range(S5_PH):
            for t in range(S5_R):
                y_s[j, pl.ds(S5_R * ph + t, J, stride=span), :] = y[ph * J:(ph + 1) * J, t * LANES:(t + 1) * LANES]
    tick(2)
    yb = jnp.concatenate([y_s[j] for j in range(S5_SLABS)], axis=1) + d_ref[...] * pb_ref[:, 0:D_BRANCH]
    yb = _gelu_tanh(yb)
    glu = _dot(_bf(yb), gw_ref[...])
    tick(2)
    yb = yb * _sigmoid(glu + gb_ref[...])
    ycat_ref[:, D_BRANCH:2 * D_BRANCH] = _bf(yb * _silu(pb_ref[:, D_BRANCH:2 * D_BRANCH]))


def _mixer_gla(pc_ref, ycat_ref, qd_s, ki_s, ke_s, gl_s, o_s, st_s, wg_ref, bg_ref, nw_ref, tick):
    k = pc_ref[:, GLA_DK:2 * GLA_DK]
    z_off = 2 * GLA_DK + D_BRANCH
    glow = pc_ref[:, 2 * GLA_DK + 2 * D_BRANCH:SEG_C_W]
    logits = _dot(_bf(glow), wg_ref[...]) + bg_ref[...]
    tick()
    g = -_softplus(-logits) / GLA_TAU
    gc = _chunk_cumsum(g)
    tick(2)
    qd_s[...] = _bf(pc_ref[:, 0:GLA_DK] * (GLA_HK ** -0.5) * jnp.exp(gc))
    ki_s[...] = _bf(k * jnp.exp(-gc))
    gl = jnp.concatenate(
        [jnp.broadcast_to(gc[c * CHUNK + CHUNK - 1:c * CHUNK + CHUNK, :], (CHUNK, GLA_DK)) for c in range(NCHUNK)], axis=0)
    ke_s[...] = _bf(k * jnp.exp(gl - gc))
    for c in range(NCHUNK):
        gl_s[c * SUBLANES:(c + 1) * SUBLANES, :] = jnp.exp(gl[c * CHUNK:c * CHUNK + SUBLANES, :])

    lane = _lane_iota((1, LANES))
    m_lo = (lane < GLA_HK).astype(BF16)
    m_hi = (lane >= GLA_HK).astype(BF16)
    tri = (_lane_iota((CHUNK, CHUNK)) <= _row_iota((CHUNK, CHUNK)))

    for c in range(NCHUNK):
        if c % 2 == 0:
            tick()
        r0 = c * CHUNK
        rows = pl.ds(r0, CHUNK)
        for tl in range(GLA_DK // LANES):
            cols = slice(tl * LANES, (tl + 1) * LANES)
            qd = qd_s[rows, cols]
            ki = ki_s[rows, cols]
            ke = ke_s[rows, cols]
            dec = gl_s[pl.ds(c * SUBLANES, 1), cols]
            qstack = jnp.concatenate([qd * m_lo, qd * m_hi], axis=0)
            s2 = _dot_nt(qstack, ki)
            for hh in range(2):
                h = 2 * tl + hh
                vh = _bf(pc_ref[rows, 2 * GLA_DK + h * GLA_HV:2 * GLA_DK + (h + 1) * GLA_HV])
                sc = jnp.where(tri, s2[hh * CHUNK:(hh + 1) * CHUNK, :], 0.0)
                qh = qstack[hh * CHUNK:(hh + 1) * CHUNK, :]
                st = st_s[h]
                o = _dot(_bf(sc), vh) + _dot_nt(qh, _bf(st))
                o_s[h, rows, :] = o
                st_s[h] = st * dec + _dot_tn(vh, ke)

    tick()
    for h in range(GLA_HEADS):
        o = o_s[h]
        o = o * lax.rsqrt(jnp.mean(o * o, axis=-1, keepdims=True) + EPS) * nw_ref[...]
        zh = pc_ref[:, z_off + h * GLA_HV:z_off + (h + 1) * GLA_HV]
        ycat_ref[:, 2 * D_BRANCH + h * GLA_HV:2 * D_BRANCH + (h + 1) * GLA_HV] = _bf(o * _silu(zh))


def _mixer_ssd(pd_ref, ycat_ref, xbc_s, ac_s, dt_s, y_s, st_s,
               cw_ref, cb_ref, dtb_ref, alog_ref, dsk_ref, nw_ref, tick):
    zt = D_BRANCH // LANES
    xt = 2 * D_BRANCH // LANES
    body = slice(SUBLANES, SUBLANES + TILE)
    tick(2)
    for k in range(xt):
        for r, ur in enumerate(_phase_conv(pd_ref, zt + k, cw_ref, cb_ref, k * LANES)):
            xbc_s[k, pl.ds(r, NPH, stride=SUBLANES), :] = _silu(ur)
    dt = _softplus(pd_ref[zt + xt, body, :] + dtb_ref[...])
    a = -jnp.exp(alog_ref[...])
    dt_s[...] = dt
    ac_s[...] = _chunk_cumsum(dt * a)
    tick()

    lane = _lane_iota((1, LANES))
    lo = lane < SSD_HDIM
    m_lo = lo.astype(F32)
    m_hi = 1.0 - m_lo
    lane_j = jnp.where(lo, lane, lane - SSD_HDIM)
    tri2 = lane_j <= _row_iota((CHUNK, LANES))
    npairs = SSD_HEADS // 2
    ppg = npairs // SSD_GROUPS

    for c in range(NCHUNK):
        tick()
        r0 = c * CHUNK
        rows = pl.ds(r0, CHUNK)
        ac = ac_s[rows, :]
        dtc = dt_s[rows, :]
        a_last = ac_s[pl.ds(r0 + CHUNK - 1, 1), :]
        dend = jnp.exp(a_last - ac) * dtc
        cdec = jnp.exp(a_last)
        act = jnp.concatenate([ac, ac], axis=0).T
        dtt = jnp.concatenate([dtc, dtc], axis=0).T
        for g in range(SSD_GROUPS):
            bm = xbc_s[npairs + g, rows, :]
            cm = xbc_s[npairs + SSD_GROUPS + g, rows, :]
            bmb = _bf(bm)
            cb2 = _dot_nt(_bf(cm), jnp.concatenate([bmb, bmb], axis=0))
            st = st_s[g]
            y_off = _dot(_bf(cm), _bf(st))
            xd, cdrow = [], []
            for pp in range(ppg):
                pr = g * ppg + pp
                h0, h1 = 2 * pr, 2 * pr + 1
                xp = xbc_s[pr, rows, :]
                acol = jnp.where(lo, jnp.broadcast_to(ac[:, h0:h0 + 1], (CHUNK, LANES)),
                                 jnp.broadcast_to(ac[:, h1:h1 + 1], (CHUNK, LANES)))
                arow = jnp.where(lo, act[h0:h0 + 1, :], act[h1:h1 + 1, :])
                dtrow = jnp.where(lo, dtt[h0:h0 + 1, :], dtt[h1:h1 + 1, :])
                seg = jnp.where(tri2, jnp.exp(jnp.where(tri2, acol - arow, 0.0)), 0.0)
                m = cb2 * seg * dtrow
                xbd = jnp.concatenate([xp * m_lo, xp * m_hi], axis=0)
                y = _dot(_bf(m), _bf(xbd))
                y = y + y_off[:, pp * LANES:(pp + 1) * LANES] * jnp.exp(acol)
                y = y + dsk_ref[:, pr * LANES:(pr + 1) * LANES] * xp
                y_s[pr, rows, :] = y
                dcol = jnp.where(lo, jnp.broadcast_to(dend[:, h0:h0 + 1], (CHUNK, LANES)),
                                 jnp.broadcast_to(dend[:, h1:h1 + 1], (CHUNK, LANES)))
                xd.append(xp * dcol)
                cdrow.append(jnp.where(lo, jnp.broadcast_to(cdec[:, h0:h0 + 1], (1, LANES)),
                                       jnp.broadcast_to(cdec[:, h1:h1 + 1], (1, LANES))))
            snew = _dot_tn(bmb, _bf(jnp.concatenate(xd, axis=1)))
            st_s[g] = st * jnp.concatenate(cdrow, axis=1) + snew

    z = jnp.concatenate([pd_ref[k, body, :] for k in range(zt)], axis=1)
    y = jnp.concatenate([y_s[pr] for pr in range(npairs)], axis=1) * _silu(z)
    _keep_tail(pd_ref)
    y = y * lax.rsqrt(jnp.mean(y * y, axis=-1, keepdims=True) + EPS) * nw_ref[...]
    ycat_ref[:, 3 * D_BRANCH:4 * D_BRANCH] = _bf(y)


class _Interleave:
    def __init__(self):
        self.queue = []

    def load(self, thunks):
        self.flush()
        self.queue = list(thunks)

    def tick(self, n=1):
        for _ in range(n):
            if self.queue:
                self.queue.pop(0)()

    def flush(self):
        while self.queue:
            self.queue.pop(0)()


PROJ_CHUNK = 256


def _layer_kernel(final, tiles_per_seq,
                  x_ref, nw_ref, win_ref, wind_ref,
                  a_cw, a_cb, a_wg, a_bg, a_l,
                  b_toep, b_eb, b_f, b_pw, b_d, b_gw, b_gb,
                  c_wg, c_bg, c_nw,
                  d_cw, d_cb, d_dtb, d_alog, d_dsk, d_nw,
                  wout_ref, nf_ref,
                  o_ref,
                  pa_s, pb_s, pc_s, pd_s, xs_s, hb_s,
                  ycat, buf0, buf1, p2_s, l2_s, c2_s, lru_carry, s5_carry,
                  qd_s, ki_s, ke_s, gl_s, gla_st, xbc_s, ac_s, dt_s, ssd_st):
    g = pl.program_id(0)

    @pl.when(g == 0)
    def _():
        pa_s[...] = jnp.zeros_like(pa_s)
        pb_s[...] = jnp.zeros_like(pb_s)
        pc_s[...] = jnp.zeros_like(pc_s)
        pd_s[...] = jnp.zeros_like(pd_s)
        xs_s[...] = jnp.zeros_like(xs_s)
        hb_s[...] = jnp.zeros_like(hb_s)

    @pl.when((g == 0) | (lax.rem(g - 1, tiles_per_seq) == 0))
    def _():
        pa_s[:, 0:SUBLANES, :] = jnp.zeros((pa_s.shape[0], SUBLANES, LANES), F32)
        pd_s[:, 0:SUBLANES, :] = jnp.zeros((pd_s.shape[0], SUBLANES, LANES), F32)
        lru_carry[...] = jnp.zeros_like(lru_carry)
        s5_carry[...] = jnp.zeros_like(s5_carry)
        gla_st[...] = jnp.zeros_like(gla_st)
        ssd_st[...] = jnp.zeros_like(ssd_st)

    def proj_chunks(w_ref, w_off, dst_ref, width, tiled):
        out = []
        for c0 in range(0, width, PROJ_CHUNK):
            c1 = min(c0 + PROJ_CHUNK, width)

            def thunk(c0=c0, c1=c1):
                res = _dot(hb_s[...], w_ref[:, w_off + c0:w_off + c1])
                if tiled:
                    for i in range((c1 - c0) // LANES):
                        dst_ref[c0 // LANES + i, SUBLANES:SUBLANES + TILE, :] = res[:, i * LANES:(i + 1) * LANES]
                else:
                    dst_ref[:, c0:c1] = res
            out.append(thunk)
        return out

    def out_chunks(m):
        out = []
        rows = slice(m * D_BRANCH, (m + 1) * D_BRANCH)
        for c0 in range(0, D_MODEL, PROJ_CHUNK):
            cols = slice(c0, c0 + PROJ_CHUNK)

            def thunk(cols=cols):
                base = xs_s[:, cols] if m == 0 else o_ref[:, cols]
                o_ref[:, cols] = base + _dot(ycat[:, rows], wout_ref[rows, cols])
            out.append(thunk)
        return out

    il = _Interleave()
    il.load(proj_chunks(wind_ref, 0, pd_s, SEG_D_W, True))
    _mixer_lru(pa_s, ycat, buf0, buf1, p2_s, l2_s, c2_s, lru_carry, a_cw, a_cb, a_wg, a_bg, a_l, il.tick)
    il.flush()
    hb_s[...] = _bf(_rmsnorm(x_ref[...], nw_ref[...]))
    il.load(proj_chunks(win_ref, SEG_A, pa_s, SEG_B - SEG_A, True) + out_chunks(0))
    _mixer_s5(pb_s, ycat, buf0, buf1, s5_carry, b_toep, b_eb, b_f, b_pw, b_d, b_gw, b_gb, il.tick)
    il.load(proj_chunks(win_ref, SEG_B, pb_s, SEG_C - SEG_B, False) + out_chunks(1))
    _mixer_gla(pc_s, ycat, qd_s, ki_s, ke_s, gl_s, buf1, gla_st, c_wg, c_bg, c_nw, il.tick)
    il.load(proj_chunks(win_ref, SEG_C, pc_s, SEG_C_W, False) + out_chunks(2))
    _mixer_ssd(pd_s, ycat, xbc_s, ac_s, dt_s, buf1, ssd_st, d_cw, d_cb, d_dtb, d_alog, d_dsk, d_nw, il.tick)
    il.load(out_chunks(3))
    il.flush()
    if final:
        o_ref[...] = _rmsnorm(o_ref[...], nf_ref[...])
    xs_s[...] = x_ref[...]


def _layer_call(layer, final, x, consts):
    bsz, seq, _ = x.shape
    assert seq % TILE == 0
    tiles_per_seq = seq // TILE
    ntiles = bsz * tiles_per_seq

    def wspec(arr):
        nd = arr.ndim - 1
        return pl.BlockSpec((None,) + arr.shape[1:], lambda g, _n=nd: (layer,) + (0,) * _n,
                            pipeline_mode=pl.Buffered(1))

    def cspec(arr):
        nd = arr.ndim
        return pl.BlockSpec(arr.shape, lambda g, _n=nd: (0,) * _n, pipeline_mode=pl.Buffered(1))

    def tile_index(t):
        return (t // tiles_per_seq, t % tiles_per_seq, 0)

    per_layer = consts["per_layer"]
    shared = consts["shared"]
    in_x = pl.BlockSpec((None, TILE, D_MODEL), lambda g: tile_index(jnp.minimum(g, ntiles - 1)))
    out_x = pl.BlockSpec((None, TILE, D_MODEL), lambda g: tile_index(jnp.maximum(g - 1, 0)))
    scratch = [
        pltpu.VMEM(((SEG_B - SEG_A) // LANES, SUBLANES + TILE, LANES), F32),
        pltpu.VMEM((TILE, SEG_C - SEG_B), F32),
        pltpu.VMEM((TILE, SEG_C_W), F32),
        pltpu.VMEM((SEG_D_W // LANES, SUBLANES + TILE, LANES), F32),
        pltpu.VMEM((TILE, D_MODEL), F32),
        pltpu.VMEM((TILE, D_MODEL), BF16),
        pltpu.VMEM((TILE, 4 * D_BRANCH), BF16),
        pltpu.VMEM((4, TILE, LANES), F32),
        pltpu.VMEM((4, TILE, LANES), F32),
        pltpu.VMEM((4, TILE // 8, LANES), F32),
        pltpu.VMEM((4, TILE // 8, LANES), F32),
        pltpu.VMEM((4, TILE // 8, LANES), F32),
        pltpu.VMEM((SUBLANES, D_BRANCH), F32),
        pltpu.VMEM((S5_SLABS, SUBLANES, S5_SW), F32),
        pltpu.VMEM((TILE, GLA_DK), BF16),
        pltpu.VMEM((TILE, GLA_DK), BF16),
        pltpu.VMEM((TILE, GLA_DK), BF16),
        pltpu.VMEM((NCHUNK * SUBLANES, GLA_DK), F32),
        pltpu.VMEM((GLA_HEADS, GLA_HV, LANES), F32),
        pltpu.VMEM((2 * D_BRANCH // LANES, TILE, LANES), F32),
        pltpu.VMEM((TILE, LANES), F32),
        pltpu.VMEM((TILE, LANES), F32),
        pltpu.VMEM((SSD_GROUPS, SSD_STATE, D_BRANCH // SSD_GROUPS), F32),
    ]
    return pl.pallas_call(
        functools.partial(_layer_kernel, final, tiles_per_seq),
        grid=(ntiles + 1,),
        in_specs=[in_x] + [wspec(a) for a in per_layer] + [cspec(a) for a in shared],
        out_specs=out_x,
        out_shape=jax.ShapeDtypeStruct(x.shape, F32),
        scratch_shapes=scratch,
        compiler_params=pltpu.CompilerParams(
            dimension_semantics=("arbitrary",),
            vmem_limit_bytes=VMEM_LIMIT_BYTES),
        name=f"layer{layer}",
    )(x, *per_layer, *shared)


def _prepare(norm_w, w_in, lru_conv_w, lru_conv_b, lru_w_r, lru_b_r, lru_w_i, lru_b_i, lru_l,
             s5_lam_re, s5_lam_im, s5_log_dt, s5_b_re, s5_b_im, s5_c_re, s5_c_im, s5_d, s5_glu_w, s5_glu_b,
             gla_w_gate, gla_b_gate, gla_norm_w,
             ssd_conv_w, ssd_conv_b, ssd_dt_bias, ssd_a_log, ssd_d, ssd_norm_w,
             w_out, norm_f_w):
    nl = w_in.shape[0]
    row = lambda a: a.reshape(nl, 1, -1).astype(F32)
    w_abc = jnp.pad(w_in[:, :, :ORIG_C_END], ((0, 0), (0, 0), (0, SEG_D - ORIG_C_END))).astype(BF16)
    w_d = jnp.pad(w_in[:, :, ORIG_C_END:], ((0, 0), (0, 0), (0, SEG_D_W - (ORIG_D_IN - ORIG_C_END)))).astype(BF16)
    hph = LRU_HEADS // 2
    eye = jnp.eye(hph, dtype=F32)

    def bd(w):
        w = w.reshape(nl, 2, hph, LRU_HDIM, LRU_HDIM)
        return jnp.einsum("lfhij,hk->lfhikj", w, eye).reshape(nl, 2, hph * LRU_HDIM, hph * LRU_HDIM)

    a_wg = jnp.concatenate([bd(lru_w_r), bd(lru_w_i)], axis=-1).astype(BF16)
    half = D_BRANCH // 2
    br = lru_b_r.reshape(nl, 2, half)
    bi = lru_b_i.reshape(nl, 2, half)
    a_bg = jnp.concatenate([br, bi], axis=-1).reshape(nl, 1, 2 * D_BRANCH)
    toep, ebm, fm, pw = _s5_prepare(s5_lam_re, s5_lam_im, s5_log_dt, s5_b_re, s5_b_im, s5_c_re, s5_c_im)
    c_wg = jnp.concatenate([gla_w_gate, jnp.zeros((nl, LANES - GLA_RANK, GLA_DK), gla_w_gate.dtype)], axis=1).astype(BF16)
    pad_h = lambda a: jnp.concatenate([a, jnp.zeros((nl, LANES - SSD_HEADS), a.dtype)], axis=-1).reshape(nl, 1, LANES)
    per_layer = [
        row(norm_w), w_abc, w_d,
        lru_conv_w.astype(F32), row(lru_conv_b), a_wg, a_bg, row(lru_l),
        toep, ebm, fm, pw, row(s5_d), s5_glu_w.astype(BF16), row(s5_glu_b),
        c_wg, row(gla_b_gate), row(gla_norm_w),
        ssd_conv_w.astype(F32), row(ssd_conv_b), pad_h(ssd_dt_bias), pad_h(ssd_a_log),
        row(jnp.repeat(ssd_d, SSD_HDIM, axis=-1)), row(ssd_norm_w),
        w_out.astype(BF16),
    ]
    shared = [norm_f_w.reshape(1, D_MODEL).astype(F32)]
    return {"per_layer": per_layer, "shared": shared}


def kernel(x, norm_w, w_in, lru_conv_w, lru_conv_b, lru_w_r, lru_b_r, lru_w_i, lru_b_i, lru_l, s5_lam_re, s5_lam_im, s5_log_dt, s5_b_re, s5_b_im, s5_c_re, s5_c_im, s5_d, s5_glu_w, s5_glu_b, gla_w_gate, gla_b_gate, gla_norm_w, ssd_conv_w, ssd_conv_b, ssd_dt_bias, ssd_a_log, ssd_d, ssd_norm_w, w_out, norm_f_w):
    consts = _prepare(norm_w, w_in, lru_conv_w, lru_conv_b, lru_w_r, lru_b_r, lru_w_i, lru_b_i, lru_l,
                      s5_lam_re, s5_lam_im, s5_log_dt, s5_b_re, s5_b_im, s5_c_re, s5_c_im, s5_d, s5_glu_w, s5_glu_b,
                      gla_w_gate, gla_b_gate, gla_norm_w,
                      ssd_conv_w, ssd_conv_b, ssd_dt_bias, ssd_a_log, ssd_d, ssd_norm_w,
                      w_out, norm_f_w)
    nl = w_in.shape[0]
    for layer in range(nl):
        x = _layer_call(layer, layer == nl - 1, x, consts)
    return x
```

```python
import functools
import math

import jax
import jax.numpy as jnp
from jax import lax
from jax.experimental import pallas as pl
from jax.experimental.pallas import tpu as pltpu

F32 = jnp.float32
BF16 = jnp.bfloat16

D_MODEL = 1024
D_BRANCH = 512
CONV_W = 4
EPS = 1e-6
LRU_HEADS = 8
LRU_HDIM = 64
LRU_C = 8.0
S5_GROUP = 16
S5_GROUPS = 32
S5_STATE = 64
GLA_HEADS = 4
GLA_DK = 256
GLA_HK = 64
GLA_HV = 128
GLA_RANK = 16
GLA_TAU = 16.0
SSD_HEADS = 8
SSD_HDIM = 64
SSD_GROUPS = 2
SSD_STATE = 128
CHUNK = 64

LANES = 128
SUBLANES = 8
VMEM_LIMIT_BYTES = 62 * 1024 * 1024

TILE = 512
NCHUNK = TILE // CHUNK
S5_R = 4
S5_PH = 8
S5_NB = TILE // S5_R
S5_J = S5_NB // S5_PH
S5_SLABS = D_BRANCH // LANES
S5_GPS = LANES // S5_GROUP
S5_SW = S5_GPS * S5_STATE
S5_SPAN = S5_PH * S5_R
S5_PITCH = S5_SPAN + SUBLANES
S5_ROWS = S5_J * S5_PITCH
S5_HS_STEPS = tuple(k for k in (1, 2, 4, 8, 16, 32) if k < S5_J)
PW_L1 = 0
PW_HS = S5_PH - 1
PW_CARRY = PW_HS + len(S5_HS_STEPS)
PW_ROWS = PW_CARRY + S5_J

SEG_A = 0
SEG_B = 1024
SEG_C = 2048
SEG_C_W = 1664
SEG_D = SEG_C + SEG_C_W
SEG_D_W = 1664
N_IN = SEG_D + SEG_D_W
ORIG_C_END = 3600
ORIG_D_IN = 5144


def _dot(a, b):
    return jnp.dot(a, b, preferred_element_type=F32)


def _dot_nt(a, b):
    return lax.dot_general(a, b, (((1,), (1,)), ((), ())), preferred_element_type=F32)


def _dot_tn(a, b):
    return lax.dot_general(a, b, (((0,), (0,)), ((), ())), preferred_element_type=F32)


def _bf(x):
    return x.astype(BF16)


def _softplus(x):
    return jnp.maximum(x, 0.0) + jnp.log1p(jnp.exp(-jnp.abs(x)))


def _sigmoid(x):
    return 0.5 * jnp.tanh(0.5 * x) + 0.5


def _silu(x):
    return x * _sigmoid(x)


def _gelu_tanh(x):
    c = math.sqrt(2.0 / math.pi)
    return 0.5 * x * (1.0 + jnp.tanh(c * (x + 0.044715 * (x * x * x))))


def _cmul(ar, ai, br, bi):
    return ar * br - ai * bi, ar * bi + ai * br


def _split_hi_lo(x):
    hi = x.astype(BF16)
    lo = (x - hi.astype(F32)).astype(BF16)
    return hi, lo


def _chunk_cumsum(x):
    hi, lo = _split_hi_lo(x)
    lo2 = (x - hi.astype(F32) - lo.astype(F32)).astype(BF16)
    ri = lax.broadcasted_iota(jnp.int32, (CHUNK, CHUNK), 0)
    ci = lax.broadcasted_iota(jnp.int32, (CHUNK, CHUNK), 1)
    t = (ci <= ri).astype(BF16)
    out = []
    for c in range(x.shape[0] // CHUNK):
        rows = slice(c * CHUNK, (c + 1) * CHUNK)
        out.append(_dot(t, hi[rows]) + _dot(t, lo[rows]) + _dot(t, lo2[rows]))
    return jnp.concatenate(out, axis=0)


def _s5_prep_kernel(lr_ref, li_ref, ldt_ref, lrc_ref, lic_ref, bt_re_ref, bt_im_ref, c_re_ref, c_im_ref,
                    ct_re_ref, ct_im_ref, toep_ref, eb_ref, f_ref, pw_ref):
    def discretise(lr, li, dt):
        mag = jnp.exp(lr * dt)
        ab_re = mag * jnp.cos(li * dt)
        ab_im = mag * jnp.sin(li * dt)
        den = lr * lr + li * li
        nr = ab_re - 1.0
        coef_re = (nr * lr + ab_im * li) / den
        coef_im = (ab_im * lr - nr * li) / den
        return ab_re, ab_im, coef_re, coef_im

    def powers(ab_re, ab_im, n):
        out = [(jnp.ones_like(ab_re), jnp.zeros_like(ab_re)), (ab_re, ab_im)]
        for _ in range(2, n + 1):
            out.append(_cmul(out[-1][0], out[-1][1], ab_re, ab_im))
        return out

    ab_re, ab_im, cf_re, cf_im = discretise(lr_ref[...], li_ref[...], jnp.exp(ldt_ref[...]))
    low = powers(ab_re, ab_im, S5_R)
    l1 = low[S5_R]
    l1_pows = [l1]
    for _ in range(2, S5_PH + 1):
        l1_pows.append(_cmul(l1_pows[-1][0], l1_pows[-1][1], l1[0], l1[1]))
    l2 = l1_pows[S5_PH - 1]
    for r in range(1, S5_PH):
        pw_ref[0, PW_L1 + r - 1] = l1_pows[r - 1][0]
        pw_ref[1, PW_L1 + r - 1] = l1_pows[r - 1][1]
    sq = l2
    for i, k in enumerate(S5_HS_STEPS):
        pw_ref[0, PW_HS + i] = sq[0]
        pw_ref[1, PW_HS + i] = sq[1]
        sq = _cmul(sq[0], sq[1], sq[0], sq[1])
    cur = l2
    for j in range(S5_J):
        pw_ref[0, PW_CARRY + j] = cur[0]
        pw_ref[1, PW_CARRY + j] = cur[1]
        cur = _cmul(cur[0], cur[1], l2[0], l2[1])

    toep_ref[...] = jnp.zeros_like(toep_ref)
    eb_ref[...] = jnp.zeros_like(eb_ref)
    f_ref[...] = jnp.zeros_like(f_ref)
    hi = lax.Precision.HIGHEST
    for g in range(S5_GPS):
        row_pw = [(p[0][g:g + 1, :], p[1][g:g + 1, :]) for p in low]
        bbt_re, bbt_im = _cmul(cf_re[g:g + 1, :], cf_im[g:g + 1, :], bt_re_ref[g], bt_im_ref[g])
        for t in range(S5_R):
            e_re, e_im = _cmul(row_pw[S5_R - 1 - t][0], row_pw[S5_R - 1 - t][1], bbt_re, bbt_im)
            rows = slice(t * LANES + g * S5_GROUP, t * LANES + (g + 1) * S5_GROUP)
            eb_ref[rows, g * S5_STATE:(g + 1) * S5_STATE] = _bf(e_re)
            eb_ref[rows, S5_SW + g * S5_STATE:S5_SW + (g + 1) * S5_STATE] = _bf(e_im)
        for k in range(S5_R):
            cl_re, cl_im = _cmul(c_re_ref[g], c_im_ref[g], row_pw[k][0], row_pw[k][1])
            kt = (lax.dot_general(bbt_re, cl_re, (((1,), (1,)), ((), ())), preferred_element_type=F32, precision=hi)
                  - lax.dot_general(bbt_im, cl_im, (((1,), (1,)), ((), ())), preferred_element_type=F32, precision=hi))
            ktb = _bf(kt)
            for t in range(S5_R - k):
                r = t + k
                toep_ref[t * LANES + g * S5_GROUP:t * LANES + (g + 1) * S5_GROUP,
                         r * LANES + g * S5_GROUP:r * LANES + (g + 1) * S5_GROUP] = ktb
        abc_re, abc_im, _, _ = discretise(lrc_ref[g], lic_ref[g], jnp.exp(ldt_ref[g:g + 1, :]))
        col_pw = powers(abc_re, abc_im, S5_R)
        for r in range(S5_R):
            f_re, f_im = _cmul(ct_re_ref[g], ct_im_ref[g], col_pw[r + 1][0], col_pw[r + 1][1])
            cols = slice(r * LANES + g * S5_GROUP, r * LANES + (g + 1) * S5_GROUP)
            f_ref[g * S5_STATE:(g + 1) * S5_STATE, cols] = _bf(f_re)
            f_ref[S5_SW + g * S5_STATE:S5_SW + (g + 1) * S5_STATE, cols] = _bf(-f_im)


def _s5_prepare(lam_re, lam_im, log_dt, b_re, b_im, c_re, c_im):
    nl = lam_re.shape[0]
    g8 = S5_GPS
    rl = S5_R * LANES

    def im4(l, j):
        return (l, j, 0, 0)

    def im3(l, j):
        return (l, j, 0)

    gsn = pl.BlockSpec((None, g8, S5_GROUP, S5_STATE), im4)
    toep, ebm, fm, pw = pl.pallas_call(
        _s5_prep_kernel,
        grid=(nl, S5_SLABS),
        in_specs=[
            pl.BlockSpec((None, g8, S5_STATE), im3),
            pl.BlockSpec((None, g8, S5_STATE), im3),
            pl.BlockSpec((None, g8, 1), im3),
            pl.BlockSpec((None, g8, S5_STATE, 1), im4),
            pl.BlockSpec((None, g8, S5_STATE, 1), im4),
            gsn, gsn, gsn, gsn,
            pl.BlockSpec((None, g8, S5_STATE, S5_GROUP), im4),
            pl.BlockSpec((None, g8, S5_STATE, S5_GROUP), im4),
        ],
        out_specs=[
            pl.BlockSpec((None, None, rl, rl), im4),
            pl.BlockSpec((None, None, rl, 2 * S5_SW), im4),
            pl.BlockSpec((None, None, 2 * S5_SW, rl), im4),
            pl.BlockSpec((None, None, 2, PW_ROWS, g8, S5_STATE), lambda l, j: (l, j, 0, 0, 0, 0)),
        ],
        out_shape=[
            jax.ShapeDtypeStruct((nl, S5_SLABS, rl, rl), BF16),
            jax.ShapeDtypeStruct((nl, S5_SLABS, rl, 2 * S5_SW), BF16),
            jax.ShapeDtypeStruct((nl, S5_SLABS, 2 * S5_SW, rl), BF16),
            jax.ShapeDtypeStruct((nl, S5_SLABS, 2, PW_ROWS, g8, S5_STATE), F32),
        ],
        name="s5_prepare",
    )(lam_re, lam_im, log_dt[..., None], lam_re[..., None], lam_im[..., None],
      jnp.swapaxes(b_re, -1, -2), jnp.swapaxes(b_im, -1, -2), c_re, c_im,
      jnp.swapaxes(c_re, -1, -2), jnp.swapaxes(c_im, -1, -2))
    return toep, ebm, fm, pw.reshape(nl, S5_SLABS, 2, PW_ROWS, S5_SW)


def _rmsnorm(x, w):
    return x * lax.rsqrt(jnp.mean(x * x, axis=-1, keepdims=True) + EPS) * w


NPH = TILE // SUBLANES


def _phase_conv(p_ref, tile, w_ref, b_ref, c0):
    cols = slice(c0, c0 + LANES)
    w = w_ref[:, cols]
    b = b_ref[:, cols]
    xs = {r: p_ref[tile, pl.ds(SUBLANES + r, NPH, stride=SUBLANES), :] for r in range(1 - CONV_W, SUBLANES)}
    out = []
    for r in range(SUBLANES):
        acc = b + w[CONV_W - 1:CONV_W, :] * xs[r]
        for k in range(CONV_W - 1):
            acc = acc + w[k:k + 1, :] * xs[r - (CONV_W - 1) + k]
        out.append(acc)
    return out


def _keep_tail(p_ref):
    p_ref[:, 0:SUBLANES, :] = p_ref[:, TILE:TILE + SUBLANES, :]


def _row_iota(shape):
    return lax.broadcasted_iota(jnp.int32, shape, 0)


def _lane_iota(shape):
    return lax.broadcasted_iota(jnp.int32, shape, 1)


def _lru_scan(a_s, b_s, p2_s, l2_s, c2_s, carry_ref, tick):
    n1 = TILE // 8
    n2 = n1 // 8
    assert n2 == SUBLANES
    row = _row_iota((SUBLANES, LANES))
    for k in range(D_BRANCH // LANES):
        tick()

        def ph1(ref, r):
            return ref.at[k, pl.ds(r, n1, stride=8), :]

        def ph2(ref, r):
            return ref.at[k, pl.ds(r, n2, stride=8), :]

        p = ph1(a_s, 0)[...]
        l = ph1(b_s, 0)[...]
        for r in range(1, 8):
            ar = ph1(a_s, r)[...]
            br = ph1(b_s, r)[...]
            l = ar * l + br
            p = ar * p
            ph1(a_s, r)[...] = p
            ph1(b_s, r)[...] = l
        p2_s[k] = p
        l2_s[k] = l
        p2 = [ph2(p2_s, 0)[...]]
        l2 = [ph2(l2_s, 0)[...]]
        for r in range(1, 8):
            ar = ph2(p2_s, r)[...]
            br = ph2(l2_s, r)[...]
            l2.append(ar * l2[-1] + br)
            p2.append(ar * p2[-1])
        p3, l3 = p2[7], l2[7]
        for sh in (1, 2, 4):
            psh = jnp.where(row >= sh, pltpu.roll(p3, sh, 0), 1.0)
            lsh = jnp.where(row >= sh, pltpu.roll(l3, sh, 0), 0.0)
            l3 = p3 * lsh + l3
            p3 = p3 * psh
        cin = carry_ref[:, k * LANES:(k + 1) * LANES]
        c3 = l3 + p3 * cin
        c3prev = jnp.where(row >= 1, pltpu.roll(c3, 1, 0), cin)
        carry_ref[:, k * LANES:(k + 1) * LANES] = jnp.broadcast_to(c3[SUBLANES - 1:SUBLANES, :], (SUBLANES, LANES))
        ph2(c2_s, 0)[...] = c3prev
        for r in range(1, 8):
            ph2(c2_s, r)[...] = l2[r - 1] + p2[r - 1] * c3prev
        cprev = c2_s[k]
        for r in range(8):
            ph1(b_s, r)[...] = ph1(b_s, r)[...] + ph1(a_s, r)[...] * cprev


def _mixer_lru(pa_ref, ycat_ref, a_s, b_s, p2_s, l2_s, c2_s, carry_ref,
               cw_ref, cb_ref, wg_ref, bg_ref, ll_ref, tick):
    tick()
    ntile = D_BRANCH // LANES
    for k in range(ntile):
        for r, ur in enumerate(_phase_conv(pa_ref, k, cw_ref, cb_ref, k * LANES)):
            a_s[k, pl.ds(r, NPH, stride=SUBLANES), :] = ur
    u = jnp.concatenate([a_s[k, 0:TILE, :] for k in range(ntile)], axis=1)
    ub = _bf(u)
    half = D_BRANCH // 2
    rs, is_ = [], []
    for hf in range(2):
        lg = _dot(ub[:, hf * half:(hf + 1) * half], wg_ref[hf]) + bg_ref[:, hf * D_BRANCH:(hf + 1) * D_BRANCH]
        tick()
        rs.append(_sigmoid(lg[:, 0:half]))
        is_.append(_sigmoid(lg[:, half:2 * half]))
    r = jnp.concatenate(rs, axis=1)
    i = jnp.concatenate(is_, axis=1)
    coef = -LRU_C * _softplus(-ll_ref[...])
    a = jnp.exp(coef * r)
    om = 1.0 - a * a
    mult = jnp.where(om > 0.0, om * lax.rsqrt(om), 0.0)
    b = mult * i * u
    for k in range(D_BRANCH // LANES):
        a_s[k, 0:TILE, :] = a[:, k * LANES:(k + 1) * LANES]
        b_s[k, 0:TILE, :] = b[:, k * LANES:(k + 1) * LANES]
    _lru_scan(a_s, b_s, p2_s, l2_s, c2_s, carry_ref, tick)
    for k in range(D_BRANCH // LANES):
        ycat_ref[:, k * LANES:(k + 1) * LANES] = _bf(
            b_s[k, 0:TILE, :] * _silu(pa_ref[ntile + k, SUBLANES:SUBLANES + TILE, :]))
    _keep_tail(pa_ref)


def _shift_rows(c, k, row):
    n = c.shape[0]
    if k % SUBLANES == 0:
        return jnp.concatenate([jnp.zeros((k, c.shape[1]), c.dtype), c[:n - k]], axis=0)
    return jnp.where(row >= k, pltpu.roll(c, k, 0), 0.0)


def _mixer_s5(pb_ref, ycat_ref, u_s, y_s, carry_ref, toep_ref, eb_ref, f_ref, pw_ref, d_ref, gw_ref, gb_ref, tick):
    for j in range(S5_SLABS):
        for sb in range(S5_J):
            u_s[j, sb * S5_PITCH:sb * S5_PITCH + S5_SPAN, :] = pb_ref[sb * S5_SPAN:(sb + 1) * S5_SPAN, j * LANES:(j + 1) * LANES]
    J = S5_J
    row = _row_iota((J, S5_SW))
    span = S5_PITCH
    for j in range(S5_SLABS):
        lhs = jnp.concatenate(
            [jnp.concatenate([u_s[j, pl.ds(S5_R * ph + t, J, stride=span), :] for t in range(S5_R)], axis=1)
             for ph in range(S5_PH)], axis=0)
        lhs = _bf(lhs)
        y_loc = _dot(lhs, toep_ref[j])
        e = _dot(lhs, eb_ref[j])
        tick()

        def pw(i):
            return pw_ref[j, 0, i:i + 1, :], pw_ref[j, 1, i:i + 1, :]

        l1r, l1i = pw(PW_L1)
        lr_ = [e[0:J, 0:S5_SW]]
        li_ = [e[0:J, S5_SW:2 * S5_SW]]
        for ph in range(1, S5_PH):
            mr, mi = _cmul(lr_[-1], li_[-1], l1r, l1i)
            lr_.append(mr + e[ph * J:(ph + 1) * J, 0:S5_SW])
            li_.append(mi + e[ph * J:(ph + 1) * J, S5_SW:2 * S5_SW])
        cr, ci = lr_[-1], li_[-1]
        for idx, k in enumerate(S5_HS_STEPS):
            qr, qi = pw(PW_HS + idx)
            mr, mi = _cmul(_shift_rows(cr, k, row), _shift_rows(ci, k, row), qr, qi)
            cr, ci = cr + mr, ci + mi
        cin_r = carry_ref[j, 0:1, :]
        cin_i = carry_ref[j, 1:2, :]
        tr = pw_ref[j, 0, PW_CARRY:PW_CARRY + J, :]
        ti = pw_ref[j, 1, PW_CARRY:PW_CARRY + J, :]
        mr, mi = _cmul(tr, ti, cin_r, cin_i)
        cr, ci = cr + mr, ci + mi
        carry_ref[j, 0:1, :] = cr[J - 1:J, :]
        carry_ref[j, 1:2, :] = ci[J - 1:J, :]
        cpr = jnp.where(row >= 1, pltpu.roll(cr, 1, 0), cin_r)
        cpi = jnp.where(row >= 1, pltpu.roll(ci, 1, 0), cin_i)
        sp_r, sp_i = [cpr], [cpi]
        for ph in range(1, S5_PH):
            qr, qi = pw(PW_L1 + ph - 1)
            mr, mi = _cmul(cpr, cpi, qr, qi)
            sp_r.append(lr_[ph - 1] + mr)
            sp_i.append(li_[ph - 1] + mi)
        sprev = jnp.concatenate([jnp.concatenate(sp_r, axis=0), jnp.concatenate(sp_i, axis=0)], axis=1)
        y = y_loc + _dot(_bf(sprev), f_ref[j])
        for ph in range(S5_PH):
            for t in range(S5_R):
                y_s[j, pl.ds(S5_R * ph + t, J, stride=span), :] = y[ph * J:(ph + 1) * J, t * LANES:(t + 1) * LANES]
    tick(2)
    yb = jnp.concatenate(
        [jnp.concatenate([y_s[j, sb * S5_PITCH:sb * S5_PITCH + S5_SPAN, :] for sb in range(S5_J)], axis=0)
         for j in range(S5_SLABS)], axis=1) + d_ref[...] * pb_ref[:, 0:D_BRANCH]
    yb = _gelu_tanh(yb)
    glu = _dot(_bf(yb), gw_ref[...])
    tick(2)
    yb = yb * _sigmoid(glu + gb_ref[...])
    ycat_ref[:, D_BRANCH:2 * D_BRANCH] = _bf(yb * _silu(pb_ref[:, D_BRANCH:2 * D_BRANCH]))


def _mixer_gla(pc_ref, ycat_ref, qd_s, ki_s, ke_s, gl_s, o_s, st_s, wg_ref, bg_ref, nw_ref, tick):
    k = pc_ref[:, GLA_DK:2 * GLA_DK]
    z_off = 2 * GLA_DK + D_BRANCH
    glow = pc_ref[:, 2 * GLA_DK + 2 * D_BRANCH:SEG_C_W]
    logits = _dot(_bf(glow), wg_ref[...]) + bg_ref[...]
    tick()
    g = -_softplus(-logits) / GLA_TAU
    gc = _chunk_cumsum(g)
    tick(2)
    qd_s[...] = _bf(pc_ref[:, 0:GLA_DK] * (GLA_HK ** -0.5) * jnp.exp(gc))
    ki_s[...] = _bf(k * jnp.exp(-gc))
    gl = jnp.concatenate(
        [jnp.broadcast_to(gc[c * CHUNK + CHUNK - 1:c * CHUNK + CHUNK, :], (CHUNK, GLA_DK)) for c in range(NCHUNK)], axis=0)
    ke_s[...] = _bf(k * jnp.exp(gl - gc))
    for c in range(NCHUNK):
        gl_s[c * SUBLANES:(c + 1) * SUBLANES, :] = jnp.exp(gl[c * CHUNK:c * CHUNK + SUBLANES, :])

    lane = _lane_iota((1, LANES))
    m_lo = (lane < GLA_HK).astype(BF16)
    m_hi = (lane >= GLA_HK).astype(BF16)
    tri = (_lane_iota((CHUNK, CHUNK)) <= _row_iota((CHUNK, CHUNK)))

    for c in range(NCHUNK):
        if c % 2 == 0:
            tick()
        r0 = c * CHUNK
        rows = pl.ds(r0, CHUNK)
        for tl in range(GLA_DK // LANES):
            cols = slice(tl * LANES, (tl + 1) * LANES)
            qd = qd_s[rows, cols]
            ki = ki_s[rows, cols]
            ke = ke_s[rows, cols]
            dec = gl_s[pl.ds(c * SUBLANES, 1), cols]
            qstack = jnp.concatenate([qd * m_lo, qd * m_hi], axis=0)
            s2 = _dot_nt(qstack, ki)
            for hh in range(2):
                h = 2 * tl + hh
                vh = _bf(pc_ref[rows, 2 * GLA_DK + h * GLA_HV:2 * GLA_DK + (h + 1) * GLA_HV])
                sc = jnp.where(tri, s2[hh * CHUNK:(hh + 1) * CHUNK, :], 0.0)
                qh = qstack[hh * CHUNK:(hh + 1) * CHUNK, :]
                st = st_s[h]
                o = _dot(_bf(sc), vh) + _dot_nt(qh, _bf(st))
                o_s[h, rows, :] = o
                st_s[h] = st * dec + _dot_tn(vh, ke)

    tick()
    for h in range(GLA_HEADS):
        o = o_s[h, 0:TILE, :]
        o = o * lax.rsqrt(jnp.mean(o * o, axis=-1, keepdims=True) + EPS) * nw_ref[...]
        zh = pc_ref[:, z_off + h * GLA_HV:z_off + (h + 1) * GLA_HV]
        ycat_ref[:, 2 * D_BRANCH + h * GLA_HV:2 * D_BRANCH + (h + 1) * GLA_HV] = _bf(o * _silu(zh))


def _mixer_ssd(pd_ref, ycat_ref, xbc_s, ac_s, dt_s, y_s, st_s,
               cw_ref, cb_ref, dtb_ref, alog_ref, dsk_ref, nw_ref, tick):
    zt = D_BRANCH // LANES
    xt = 2 * D_BRANCH // LANES
    body = slice(SUBLANES, SUBLANES + TILE)
    tick(2)
    for k in range(xt):
        for r, ur in enumerate(_phase_conv(pd_ref, zt + k, cw_ref, cb_ref, k * LANES)):
            xbc_s[k, pl.ds(r, NPH, stride=SUBLANES), :] = _silu(ur)
    dt = _softplus(pd_ref[zt + xt, body, :] + dtb_ref[...])
    a = -jnp.exp(alog_ref[...])
    dt_s[...] = dt
    ac_s[...] = _chunk_cumsum(dt * a)
    tick()

    lane = _lane_iota((1, LANES))
    lo = lane < SSD_HDIM
    m_lo = lo.astype(F32)
    m_hi = 1.0 - m_lo
    lane_j = jnp.where(lo, lane, lane - SSD_HDIM)
    tri2 = lane_j <= _row_iota((CHUNK, LANES))
    npairs = SSD_HEADS // 2
    ppg = npairs // SSD_GROUPS

    for c in range(NCHUNK):
        tick()
        r0 = c * CHUNK
        rows = pl.ds(r0, CHUNK)
        ac = ac_s[rows, :]
        dtc = dt_s[rows, :]
        a_last = ac_s[pl.ds(r0 + CHUNK - 1, 1), :]
        dend = jnp.exp(a_last - ac) * dtc
        cdec = jnp.exp(a_last)
        act = jnp.concatenate([ac, ac], axis=0).T
        dtt = jnp.concatenate([dtc, dtc], axis=0).T
        for g in range(SSD_GROUPS):
            bm = xbc_s[npairs + g, rows, :]
            cm = xbc_s[npairs + SSD_GROUPS + g, rows, :]
            bmb = _bf(bm)
            cb2 = _dot_nt(_bf(cm), jnp.concatenate([bmb, bmb], axis=0))
            st = st_s[g]
            y_off = _dot(_bf(cm), _bf(st))
            xd, cdrow = [], []
            for pp in range(ppg):
                pr = g * ppg + pp
                h0, h1 = 2 * pr, 2 * pr + 1
                xp = xbc_s[pr, rows, :]
                acol = jnp.where(lo, jnp.broadcast_to(ac[:, h0:h0 + 1], (CHUNK, LANES)),
                                 jnp.broadcast_to(ac[:, h1:h1 + 1], (CHUNK, LANES)))
                arow = jnp.where(lo, act[h0:h0 + 1, :], act[h1:h1 + 1, :])
                dtrow = jnp.where(lo, dtt[h0:h0 + 1, :], dtt[h1:h1 + 1, :])
                seg = jnp.where(tri2, jnp.exp(jnp.where(tri2, acol - arow, 0.0)), 0.0)
                m = cb2 * seg * dtrow
                xbd = jnp.concatenate([xp * m_lo, xp * m_hi], axis=0)
                y = _dot(_bf(m), _bf(xbd))
                y = y + y_off[:, pp * LANES:(pp + 1) * LANES] * jnp.exp(acol)
                y = y + dsk_ref[:, pr * LANES:(pr + 1) * LANES] * xp
                y_s[pr, rows, :] = y
                dcol = jnp.where(lo, jnp.broadcast_to(dend[:, h0:h0 + 1], (CHUNK, LANES)),
                                 jnp.broadcast_to(dend[:, h1:h1 + 1], (CHUNK, LANES)))
                xd.append(xp * dcol)
                cdrow.append(jnp.where(lo, jnp.broadcast_to(cdec[:, h0:h0 + 1], (1, LANES)),
                                       jnp.broadcast_to(cdec[:, h1:h1 + 1], (1, LANES))))
            snew = _dot_tn(bmb, _bf(jnp.concatenate(xd, axis=1)))
            st_s[g] = st * jnp.concatenate(cdrow, axis=1) + snew

    z = jnp.concatenate([pd_ref[k, body, :] for k in range(zt)], axis=1)
    y = jnp.concatenate([y_s[pr, 0:TILE, :] for pr in range(npairs)], axis=1) * _silu(z)
    _keep_tail(pd_ref)
    y = y * lax.rsqrt(jnp.mean(y * y, axis=-1, keepdims=True) + EPS) * nw_ref[...]
    ycat_ref[:, 3 * D_BRANCH:4 * D_BRANCH] = _bf(y)


class _Interleave:
    def __init__(self):
        self.queue = []

    def load(self, thunks):
        self.flush()
        self.queue = list(thunks)

    def tick(self, n=1):
        for _ in range(n):
            if self.queue:
                self.queue.pop(0)()

    def flush(self):
        while self.queue:
            self.queue.pop(0)()


PROJ_CHUNK = 256


def _layer_kernel(final, tiles_per_seq,
                  x_ref, nw_ref, win_ref, wind_ref,
                  a_cw, a_cb, a_wg, a_bg, a_l,
                  b_toep, b_eb, b_f, b_pw, b_d, b_gw, b_gb,
                  c_wg, c_bg, c_nw,
                  d_cw, d_cb, d_dtb, d_alog, d_dsk, d_nw,
                  wout_ref, nf_ref,
                  o_ref,
                  pa_s, pb_s, pc_s, pd_s, xs_s, hb_s,
                  ycat, buf0, buf1, p2_s, l2_s, c2_s, lru_carry, s5_carry,
                  qd_s, ki_s, ke_s, gl_s, gla_st, xbc_s, ac_s, dt_s, ssd_st):
    g = pl.program_id(0)

    @pl.when(g == 0)
    def _():
        pa_s[...] = jnp.zeros_like(pa_s)
        pb_s[...] = jnp.zeros_like(pb_s)
        pc_s[...] = jnp.zeros_like(pc_s)
        pd_s[...] = jnp.zeros_like(pd_s)
        xs_s[...] = jnp.zeros_like(xs_s)
        hb_s[...] = jnp.zeros_like(hb_s)

    @pl.when((g == 0) | (lax.rem(g - 1, tiles_per_seq) == 0))
    def _():
        pa_s[:, 0:SUBLANES, :] = jnp.zeros((pa_s.shape[0], SUBLANES, LANES), F32)
        pd_s[:, 0:SUBLANES, :] = jnp.zeros((pd_s.shape[0], SUBLANES, LANES), F32)
        lru_carry[...] = jnp.zeros_like(lru_carry)
        s5_carry[...] = jnp.zeros_like(s5_carry)
        gla_st[...] = jnp.zeros_like(gla_st)
        ssd_st[...] = jnp.zeros_like(ssd_st)

    def proj_chunks(w_ref, w_off, dst_ref, width, tiled):
        out = []
        for c0 in range(0, width, PROJ_CHUNK):
            c1 = min(c0 + PROJ_CHUNK, width)

            def thunk(c0=c0, c1=c1):
                res = _dot(hb_s[...], w_ref[:, w_off + c0:w_off + c1])
                if tiled:
                    for i in range((c1 - c0) // LANES):
                        dst_ref[c0 // LANES + i, SUBLANES:SUBLANES + TILE, :] = res[:, i * LANES:(i + 1) * LANES]
                else:
                    dst_ref[:, c0:c1] = res
            out.append(thunk)
        return out

    def out_chunks(m):
        out = []
        rows = slice(m * D_BRANCH, (m + 1) * D_BRANCH)
        for c0 in range(0, D_MODEL, PROJ_CHUNK):
            cols = slice(c0, c0 + PROJ_CHUNK)

            def thunk(cols=cols):
                base = xs_s[:, cols] if m == 0 else o_ref[:, cols]
                o_ref[:, cols] = base + _dot(ycat[:, rows], wout_ref[rows, cols])
            out.append(thunk)
        return out

    il = _Interleave()
    il.load(proj_chunks(wind_ref, 0, pd_s, SEG_D_W, True))
    _mixer_lru(pa_s, ycat, buf0, buf1, p2_s, l2_s, c2_s, lru_carry, a_cw, a_cb, a_wg, a_bg, a_l, il.tick)
    il.flush()
    hb_s[...] = _bf(_rmsnorm(x_ref[...], nw_ref[...]))
    il.load(proj_chunks(win_ref, SEG_A, pa_s, SEG_B - SEG_A, True) + out_chunks(0))
    _mixer_s5(pb_s, ycat, buf0, buf1, s5_carry, b_toep, b_eb, b_f, b_pw, b_d, b_gw, b_gb, il.tick)
    il.load(proj_chunks(win_ref, SEG_B, pb_s, SEG_C - SEG_B, False) + out_chunks(1))
    _mixer_gla(pc_s, ycat, qd_s, ki_s, ke_s, gl_s, buf1, gla_st, c_wg, c_bg, c_nw, il.tick)
    il.load(proj_chunks(win_ref, SEG_C, pc_s, SEG_C_W, False) + out_chunks(2))
    _mixer_ssd(pd_s, ycat, xbc_s, ac_s, dt_s, buf1, ssd_st, d_cw, d_cb, d_dtb, d_alog, d_dsk, d_nw, il.tick)
    il.load(out_chunks(3))
    il.flush()
    if final:
        o_ref[...] = _rmsnorm(o_ref[...], nf_ref[...])
    xs_s[...] = x_ref[...]


def _layer_call(layer, final, x, consts):
    bsz, seq, _ = x.shape
    assert seq % TILE == 0
    tiles_per_seq = seq // TILE
    ntiles = bsz * tiles_per_seq

    def wspec(arr):
        nd = arr.ndim - 1
        return pl.BlockSpec((None,) + arr.shape[1:], lambda g, _n=nd: (layer,) + (0,) * _n,
                            pipeline_mode=pl.Buffered(1))

    def cspec(arr):
        nd = arr.ndim
        return pl.BlockSpec(arr.shape, lambda g, _n=nd: (0,) * _n, pipeline_mode=pl.Buffered(1))

    def tile_index(t):
        return (t // tiles_per_seq, t % tiles_per_seq, 0)

    per_layer = consts["per_layer"]
    shared = consts["shared"]
    in_x = pl.BlockSpec((None, TILE, D_MODEL), lambda g: tile_index(jnp.minimum(g, ntiles - 1)))
    out_x = pl.BlockSpec((None, TILE, D_MODEL), lambda g: tile_index(jnp.maximum(g - 1, 0)))
    scratch = [
        pltpu.VMEM(((SEG_B - SEG_A) // LANES, SUBLANES + TILE, LANES), F32),
        pltpu.VMEM((TILE, SEG_C - SEG_B), F32),
        pltpu.VMEM((TILE, SEG_C_W), F32),
        pltpu.VMEM((SEG_D_W // LANES, SUBLANES + TILE, LANES), F32),
        pltpu.VMEM((TILE, D_MODEL), F32),
        pltpu.VMEM((TILE, D_MODEL), BF16),
        pltpu.VMEM((TILE, 4 * D_BRANCH), BF16),
        pltpu.VMEM((4, S5_ROWS, LANES), F32),
        pltpu.VMEM((4, S5_ROWS, LANES), F32),
        pltpu.VMEM((4, TILE // 8, LANES), F32),
        pltpu.VMEM((4, TILE // 8, LANES), F32),
        pltpu.VMEM((4, TILE // 8, LANES), F32),
        pltpu.VMEM((SUBLANES, D_BRANCH), F32),
        pltpu.VMEM((S5_SLABS, SUBLANES, S5_SW), F32),
        pltpu.VMEM((TILE, GLA_DK), BF16),
        pltpu.VMEM((TILE, GLA_DK), BF16),
        pltpu.VMEM((TILE, GLA_DK), BF16),
        pltpu.VMEM((NCHUNK * SUBLANES, GLA_DK), F32),
        pltpu.VMEM((GLA_HEADS, GLA_HV, LANES), F32),
        pltpu.VMEM((2 * D_BRANCH // LANES, TILE, LANES), F32),
        pltpu.VMEM((TILE, LANES), F32),
        pltpu.VMEM((TILE, LANES), F32),
        pltpu.VMEM((SSD_GROUPS, SSD_STATE, D_BRANCH // SSD_GROUPS), F32),
    ]
    return pl.pallas_call(
        functools.partial(_layer_kernel, final, tiles_per_seq),
        grid=(ntiles + 1,),
        in_specs=[in_x] + [wspec(a) for a in per_layer] + [cspec(a) for a in shared],
        out_specs=out_x,
        out_shape=jax.ShapeDtypeStruct(x.shape, F32),
        scratch_shapes=scratch,
        compiler_params=pltpu.CompilerParams(
            dimension_semantics=("arbitrary",),
            vmem_limit_bytes=VMEM_LIMIT_BYTES),
        name=f"layer{layer}",
    )(x, *per_layer, *shared)


def _prepare(norm_w, w_in, lru_conv_w, lru_conv_b, lru_w_r, lru_b_r, lru_w_i, lru_b_i, lru_l,
             s5_lam_re, s5_lam_im, s5_log_dt, s5_b_re, s5_b_im, s5_c_re, s5_c_im, s5_d, s5_glu_w, s5_glu_b,
             gla_w_gate, gla_b_gate, gla_norm_w,
             ssd_conv_w, ssd_conv_b, ssd_dt_bias, ssd_a_log, ssd_d, ssd_norm_w,
             w_out, norm_f_w):
    nl = w_in.shape[0]
    row = lambda a: a.reshape(nl, 1, -1).astype(F32)
    w_abc = jnp.pad(w_in[:, :, :ORIG_C_END], ((0, 0), (0, 0), (0, SEG_D - ORIG_C_END))).astype(BF16)
    w_d = jnp.pad(w_in[:, :, ORIG_C_END:], ((0, 0), (0, 0), (0, SEG_D_W - (ORIG_D_IN - ORIG_C_END)))).astype(BF16)
    hph = LRU_HEADS // 2
    eye = jnp.eye(hph, dtype=F32)

    def bd(w):
        w = w.reshape(nl, 2, hph, LRU_HDIM, LRU_HDIM)
        return jnp.einsum("lfhij,hk->lfhikj", w, eye).reshape(nl, 2, hph * LRU_HDIM, hph * LRU_HDIM)

    a_wg = jnp.concatenate([bd(lru_w_r), bd(lru_w_i)], axis=-1).astype(BF16)
    half = D_BRANCH // 2
    br = lru_b_r.reshape(nl, 2, half)
    bi = lru_b_i.reshape(nl, 2, half)
    a_bg = jnp.concatenate([br, bi], axis=-1).reshape(nl, 1, 2 * D_BRANCH)
    toep, ebm, fm, pw = _s5_prepare(s5_lam_re, s5_lam_im, s5_log_dt, s5_b_re, s5_b_im, s5_c_re, s5_c_im)
    c_wg = jnp.concatenate([gla_w_gate, jnp.zeros((nl, LANES - GLA_RANK, GLA_DK), gla_w_gate.dtype)], axis=1).astype(BF16)
    pad_h = lambda a: jnp.concatenate([a, jnp.zeros((nl, LANES - SSD_HEADS), a.dtype)], axis=-1).reshape(nl, 1, LANES)
    per_layer = [
        row(norm_w), w_abc, w_d,
        lru_conv_w.astype(F32), row(lru_conv_b), a_wg, a_bg, row(lru_l),
        toep, ebm, fm, pw, row(s5_d), s5_glu_w.astype(BF16), row(s5_glu_b),
        c_wg, row(gla_b_gate), row(gla_norm_w),
        ssd_conv_w.astype(F32), row(ssd_conv_b), pad_h(ssd_dt_bias), pad_h(ssd_a_log),
        row(jnp.repeat(ssd_d, SSD_HDIM, axis=-1)), row(ssd_norm_w),
        w_out.astype(BF16),
    ]
    shared = [norm_f_w.reshape(1, D_MODEL).astype(F32)]
    return {"per_layer": per_layer, "shared": shared}


def kernel(x, norm_w, w_in, lru_conv_w, lru_conv_b, lru_w_r, lru_b_r, lru_w_i, lru_b_i, lru_l, s5_lam_re, s5_lam_im, s5_log_dt, s5_b_re, s5_b_im, s5_c_re, s5_c_im, s5_d, s5_glu_w, s5_glu_b, gla_w_gate, gla_b_gate, gla_norm_w, ssd_conv_w, ssd_conv_b, ssd_dt_bias, ssd_a_log, ssd_d, ssd_norm_w, w_out, norm_f_w):
    consts = _prepare(norm_w, w_in, lru_conv_w, lru_conv_b, lru_w_r, lru_b_r, lru_w_i, lru_b_i, lru_l,
                      s5_lam_re, s5_lam_im, s5_log_dt, s5_b_re, s5_b_im, s5_c_re, s5_c_im, s5_d, s5_glu_w, s5_glu_b,
                      gla_w_gate, gla_b_gate, gla_norm_w,
                      ssd_conv_w, ssd_conv_b, ssd_dt_bias, ssd_a_log, ssd_d, ssd_norm_w,
                      w_out, norm_f_w)
    nl = w_in.shape[0]
    for layer in range(nl):
        x = _layer_call(layer, layer == nl - 1, x, consts)
    return x
```

```python
import functools
import math

import jax
import jax.numpy as jnp
from jax import lax
from jax.experimental import pallas as pl
from jax.experimental.pallas import tpu as pltpu

F32 = jnp.float32
BF16 = jnp.bfloat16

D_MODEL = 1024
D_BRANCH = 512
CONV_W = 4
EPS = 1e-6
LRU_HEADS = 8
LRU_HDIM = 64
LRU_C = 8.0
S5_GROUP = 16
S5_GROUPS = 32
S5_STATE = 64
GLA_HEADS = 4
GLA_DK = 256
GLA_HK = 64
GLA_HV = 128
GLA_RANK = 16
GLA_TAU = 16.0
SSD_HEADS = 8
SSD_HDIM = 64
SSD_GROUPS = 2
SSD_STATE = 128
CHUNK = 64

LANES = 128
SUBLANES = 8
VMEM_LIMIT_BYTES = 63 * 1024 * 1024

TILE = 512
NCHUNK = TILE // CHUNK
S5_R = 4
S5_PH = 8
S5_NB = TILE // S5_R
S5_J = S5_NB // S5_PH
S5_SLABS = D_BRANCH // LANES
S5_GPS = LANES // S5_GROUP
S5_SW = S5_GPS * S5_STATE
S5_SPAN = S5_PH * S5_R
S5_PITCH = S5_SPAN + SUBLANES
S5_ROWS = S5_J * S5_PITCH
S5_HS_STEPS = tuple(k for k in (1, 2, 4, 8, 16, 32) if k < S5_J)
PW_L1 = 0
PW_HS = S5_PH - 1
PW_CARRY = PW_HS + len(S5_HS_STEPS)
PW_ROWS = PW_CARRY + S5_J

SEG_A = 0
SEG_B = 1024
SEG_C = 2048
SEG_C_W = 1664
SEG_D = SEG_C + SEG_C_W
SEG_D_W = 1664
N_IN = SEG_D + SEG_D_W
ORIG_C_END = 3600
ORIG_D_IN = 5144


def _dot(a, b):
    return jnp.dot(a, b, preferred_element_type=F32)


def _dot_nt(a, b):
    return lax.dot_general(a, b, (((1,), (1,)), ((), ())), preferred_element_type=F32)


def _dot_tn(a, b):
    return lax.dot_general(a, b, (((0,), (0,)), ((), ())), preferred_element_type=F32)


def _bf(x):
    return x.astype(BF16)


def _softplus(x):
    return jnp.maximum(x, 0.0) + jnp.log1p(jnp.exp(-jnp.abs(x)))


def _sigmoid(x):
    return 0.5 * jnp.tanh(0.5 * x) + 0.5


def _silu(x):
    return x * _sigmoid(x)


def _gelu_tanh(x):
    c = math.sqrt(2.0 / math.pi)
    return 0.5 * x * (1.0 + jnp.tanh(c * (x + 0.044715 * (x * x * x))))


def _cmul(ar, ai, br, bi):
    return ar * br - ai * bi, ar * bi + ai * br


def _split_hi_lo(x):
    hi = x.astype(BF16)
    lo = (x - hi.astype(F32)).astype(BF16)
    return hi, lo


def _chunk_cumsum(x):
    hi, lo = _split_hi_lo(x)
    lo2 = (x - hi.astype(F32) - lo.astype(F32)).astype(BF16)
    ri = lax.broadcasted_iota(jnp.int32, (CHUNK, CHUNK), 0)
    ci = lax.broadcasted_iota(jnp.int32, (CHUNK, CHUNK), 1)
    t = (ci <= ri).astype(BF16)
    out = []
    for c in range(x.shape[0] // CHUNK):
        rows = slice(c * CHUNK, (c + 1) * CHUNK)
        out.append(_dot(t, hi[rows]) + _dot(t, lo[rows]) + _dot(t, lo2[rows]))
    return jnp.concatenate(out, axis=0)


def _s5_prep_kernel(lr_ref, li_ref, ldt_ref, lrc_ref, lic_ref, bt_re_ref, bt_im_ref, c_re_ref, c_im_ref,
                    ct_re_ref, ct_im_ref, toep_ref, eb_ref, f_ref, pw_ref):
    def discretise(lr, li, dt):
        mag = jnp.exp(lr * dt)
        ab_re = mag * jnp.cos(li * dt)
        ab_im = mag * jnp.sin(li * dt)
        den = lr * lr + li * li
        nr = ab_re - 1.0
        coef_re = (nr * lr + ab_im * li) / den
        coef_im = (ab_im * lr - nr * li) / den
        return ab_re, ab_im, coef_re, coef_im

    def powers(ab_re, ab_im, n):
        out = [(jnp.ones_like(ab_re), jnp.zeros_like(ab_re)), (ab_re, ab_im)]
        for _ in range(2, n + 1):
            out.append(_cmul(out[-1][0], out[-1][1], ab_re, ab_im))
        return out

    ab_re, ab_im, cf_re, cf_im = discretise(lr_ref[...], li_ref[...], jnp.exp(ldt_ref[...]))
    low = powers(ab_re, ab_im, S5_R)
    l1 = low[S5_R]
    l1_pows = [l1]
    for _ in range(2, S5_PH + 1):
        l1_pows.append(_cmul(l1_pows[-1][0], l1_pows[-1][1], l1[0], l1[1]))
    l2 = l1_pows[S5_PH - 1]
    for r in range(1, S5_PH):
        pw_ref[0, PW_L1 + r - 1] = l1_pows[r - 1][0]
        pw_ref[1, PW_L1 + r - 1] = l1_pows[r - 1][1]
    sq = l2
    for i, k in enumerate(S5_HS_STEPS):
        pw_ref[0, PW_HS + i] = sq[0]
        pw_ref[1, PW_HS + i] = sq[1]
        sq = _cmul(sq[0], sq[1], sq[0], sq[1])
    cur = l2
    for j in range(S5_J):
        pw_ref[0, PW_CARRY + j] = cur[0]
        pw_ref[1, PW_CARRY + j] = cur[1]
        cur = _cmul(cur[0], cur[1], l2[0], l2[1])

    toep_ref[...] = jnp.zeros_like(toep_ref)
    eb_ref[...] = jnp.zeros_like(eb_ref)
    f_ref[...] = jnp.zeros_like(f_ref)
    hi = lax.Precision.HIGHEST
    for g in range(S5_GPS):
        row_pw = [(p[0][g:g + 1, :], p[1][g:g + 1, :]) for p in low]
        bbt_re, bbt_im = _cmul(cf_re[g:g + 1, :], cf_im[g:g + 1, :], bt_re_ref[g], bt_im_ref[g])
        for t in range(S5_R):
            e_re, e_im = _cmul(row_pw[S5_R - 1 - t][0], row_pw[S5_R - 1 - t][1], bbt_re, bbt_im)
            rows = slice(t * LANES + g * S5_GROUP, t * LANES + (g + 1) * S5_GROUP)
            eb_ref[rows, g * S5_STATE:(g + 1) * S5_STATE] = _bf(e_re)
            eb_ref[rows, S5_SW + g * S5_STATE:S5_SW + (g + 1) * S5_STATE] = _bf(e_im)
        for k in range(S5_R):
            cl_re, cl_im = _cmul(c_re_ref[g], c_im_ref[g], row_pw[k][0], row_pw[k][1])
            kt = (lax.dot_general(bbt_re, cl_re, (((1,), (1,)), ((), ())), preferred_element_type=F32, precision=hi)
                  - lax.dot_general(bbt_im, cl_im, (((1,), (1,)), ((), ())), preferred_element_type=F32, precision=hi))
            ktb = _bf(kt)
            for t in range(S5_R - k):
                r = t + k
                toep_ref[t * LANES + g * S5_GROUP:t * LANES + (g + 1) * S5_GROUP,
                         r * LANES + g * S5_GROUP:r * LANES + (g + 1) * S5_GROUP] = ktb
        abc_re, abc_im, _, _ = discretise(lrc_ref[g], lic_ref[g], jnp.exp(ldt_ref[g:g + 1, :]))
        col_pw = powers(abc_re, abc_im, S5_R)
        for r in range(S5_R):
            f_re, f_im = _cmul(ct_re_ref[g], ct_im_ref[g], col_pw[r + 1][0], col_pw[r + 1][1])
            cols = slice(r * LANES + g * S5_GROUP, r * LANES + (g + 1) * S5_GROUP)
            f_ref[g * S5_STATE:(g + 1) * S5_STATE, cols] = _bf(f_re)
            f_ref[S5_SW + g * S5_STATE:S5_SW + (g + 1) * S5_STATE, cols] = _bf(-f_im)


def _s5_prepare(lam_re, lam_im, log_dt, b_re, b_im, c_re, c_im):
    nl = lam_re.shape[0]
    g8 = S5_GPS
    rl = S5_R * LANES

    def im4(l, j):
        return (l, j, 0, 0)

    def im3(l, j):
        return (l, j, 0)

    gsn = pl.BlockSpec((None, g8, S5_GROUP, S5_STATE), im4)
    toep, ebm, fm, pw = pl.pallas_call(
        _s5_prep_kernel,
        grid=(nl, S5_SLABS),
        in_specs=[
            pl.BlockSpec((None, g8, S5_STATE), im3),
            pl.BlockSpec((None, g8, S5_STATE), im3),
            pl.BlockSpec((None, g8, 1), im3),
            pl.BlockSpec((None, g8, S5_STATE, 1), im4),
            pl.BlockSpec((None, g8, S5_STATE, 1), im4),
            gsn, gsn, gsn, gsn,
            pl.BlockSpec((None, g8, S5_STATE, S5_GROUP), im4),
            pl.BlockSpec((None, g8, S5_STATE, S5_GROUP), im4),
        ],
        out_specs=[
            pl.BlockSpec((None, None, rl, rl), im4),
            pl.BlockSpec((None, None, rl, 2 * S5_SW), im4),
            pl.BlockSpec((None, None, 2 * S5_SW, rl), im4),
            pl.BlockSpec((None, None, 2, PW_ROWS, g8, S5_STATE), lambda l, j: (l, j, 0, 0, 0, 0)),
        ],
        out_shape=[
            jax.ShapeDtypeStruct((nl, S5_SLABS, rl, rl), BF16),
            jax.ShapeDtypeStruct((nl, S5_SLABS, rl, 2 * S5_SW), BF16),
            jax.ShapeDtypeStruct((nl, S5_SLABS, 2 * S5_SW, rl), BF16),
            jax.ShapeDtypeStruct((nl, S5_SLABS, 2, PW_ROWS, g8, S5_STATE), F32),
        ],
        name="s5_prepare",
    )(lam_re, lam_im, log_dt[..., None], lam_re[..., None], lam_im[..., None],
      jnp.swapaxes(b_re, -1, -2), jnp.swapaxes(b_im, -1, -2), c_re, c_im,
      jnp.swapaxes(c_re, -1, -2), jnp.swapaxes(c_im, -1, -2))
    return toep, ebm, fm, pw.reshape(nl, S5_SLABS, 2, PW_ROWS, S5_SW)


def _rmsnorm(x, w):
    return x * lax.rsqrt(jnp.mean(x * x, axis=-1, keepdims=True) + EPS) * w


CONV_PH = 4
NPH = TILE // CONV_PH


def _phase_conv(p_ref, tile, w_ref, b_ref, c0):
    cols = slice(c0, c0 + LANES)
    w = w_ref[:, cols]
    b = b_ref[:, cols]
    xs = {r: p_ref[tile, pl.ds(SUBLANES + r, NPH, stride=CONV_PH), :] for r in range(1 - CONV_W, CONV_PH)}
    out = []
    for r in range(CONV_PH):
        acc = b + w[CONV_W - 1:CONV_W, :] * xs[r]
        for k in range(CONV_W - 1):
            acc = acc + w[k:k + 1, :] * xs[r - (CONV_W - 1) + k]
        out.append(acc)
    return out


def _keep_tail(p_ref):
    p_ref[:, 0:SUBLANES, :] = p_ref[:, TILE:TILE + SUBLANES, :]


def _row_iota(shape):
    return lax.broadcasted_iota(jnp.int32, shape, 0)


def _lane_iota(shape):
    return lax.broadcasted_iota(jnp.int32, shape, 1)


SCAN_RADIX = 4
SCAN_SIZES = [TILE]
while SCAN_SIZES[-1] > SUBLANES:
    SCAN_SIZES.append(SCAN_SIZES[-1] // SCAN_RADIX)
assert SCAN_SIZES[-1] == SUBLANES
SCAN_SMALL_ROWS = sum(SCAN_SIZES[1:-1])


class _SubScratch:
    def __init__(self, ref, tile0, row0):
        self.ref, self.tile0, self.row0 = ref, tile0, row0

    @property
    def at(self):
        return self

    def __getitem__(self, idx):
        k, rows, lanes = idx
        return self.ref.at[self.tile0 + k, pl.ds(self.row0 + rows.start, rows.size, stride=rows.stride), lanes]


def _lru_scan(a_s, b_s, p2_s, l2_s, c2_s, carry_ref, tick):
    radix = SCAN_RADIX
    sizes = SCAN_SIZES
    nlev = len(sizes) - 1
    offs = [0, 0] + [sum(sizes[1:i]) for i in range(2, nlev)]
    row = _row_iota((SUBLANES, LANES))
    for k in range(D_BRANCH // LANES):
        tick()

        def phase(big, small, lev, r):
            ref = big if lev == 0 else small
            return ref.at[k, pl.ds(offs[lev] + r, sizes[lev + 1], stride=radix), :]

        def dense(small, lev):
            return small.at[k, pl.ds(offs[lev], sizes[lev]), :]

        for lev in range(nlev):
            p = phase(a_s, p2_s, lev, 0)[...]
            l = phase(b_s, l2_s, lev, 0)[...]
            for r in range(1, radix):
                ar = phase(a_s, p2_s, lev, r)[...]
                br = phase(b_s, l2_s, lev, r)[...]
                l = ar * l + br
                p = ar * p
                phase(a_s, p2_s, lev, r)[...] = p
                phase(b_s, l2_s, lev, r)[...] = l
            if lev + 1 < nlev:
                dense(p2_s, lev + 1)[...] = p
                dense(l2_s, lev + 1)[...] = l
        p3, l3 = p, l
        for sh in (1, 2, 4):
            psh = jnp.where(row >= sh, pltpu.roll(p3, sh, 0), 1.0)
            lsh = jnp.where(row >= sh, pltpu.roll(l3, sh, 0), 0.0)
            l3 = p3 * lsh + l3
            p3 = p3 * psh
        cin = carry_ref[:, k * LANES:(k + 1) * LANES]
        c3 = l3 + p3 * cin
        c3prev = jnp.where(row >= 1, pltpu.roll(c3, 1, 0), cin)
        carry_ref[:, k * LANES:(k + 1) * LANES] = jnp.broadcast_to(c3[SUBLANES - 1:SUBLANES, :], (SUBLANES, LANES))
        cprev = c3prev
        for lev in range(nlev - 1, 0, -1):
            phase(None, c2_s, lev, 0)[...] = cprev
            for r in range(1, radix):
                phase(None, c2_s, lev, r)[...] = (phase(b_s, l2_s, lev, r - 1)[...]
                                                  + phase(a_s, p2_s, lev, r - 1)[...] * cprev)
            cprev = dense(c2_s, lev)[...]
        for r in range(radix):
            phase(b_s, None, 0, r)[...] = phase(b_s, None, 0, r)[...] + phase(a_s, None, 0, r)[...] * cprev


def _mixer_lru(pa_ref, ycat_ref, a_s, b_s, p2_s, l2_s, c2_s, carry_ref,
               cw_ref, cb_ref, wg_ref, bg_ref, ll_ref, tick):
    tick()
    ntile = D_BRANCH // LANES
    for k in range(ntile):
        for r, ur in enumerate(_phase_conv(pa_ref, k, cw_ref, cb_ref, k * LANES)):
            a_s[k, pl.ds(r, NPH, stride=CONV_PH), :] = ur
    u = jnp.concatenate([a_s[k, 0:TILE, :] for k in range(ntile)], axis=1)
    ub = _bf(u)
    half = D_BRANCH // 2
    rs, is_ = [], []
    for hf in range(2):
        lg = _dot(ub[:, hf * half:(hf + 1) * half], wg_ref[hf]) + bg_ref[:, hf * D_BRANCH:(hf + 1) * D_BRANCH]
        tick()
        rs.append(_sigmoid(lg[:, 0:half]))
        is_.append(_sigmoid(lg[:, half:2 * half]))
    r = jnp.concatenate(rs, axis=1)
    i = jnp.concatenate(is_, axis=1)
    coef = -LRU_C * _softplus(-ll_ref[...])
    a = jnp.exp(coef * r)
    om = 1.0 - a * a
    mult = jnp.where(om > 0.0, om * lax.rsqrt(om), 0.0)
    b = mult * i * u
    for k in range(D_BRANCH // LANES):
        a_s[k, 0:TILE, :] = a[:, k * LANES:(k + 1) * LANES]
        b_s[k, 0:TILE, :] = b[:, k * LANES:(k + 1) * LANES]
    _lru_scan(a_s, b_s, p2_s, l2_s, c2_s, carry_ref, tick)
    for k in range(D_BRANCH // LANES):
        ycat_ref[:, k * LANES:(k + 1) * LANES] = _bf(
            b_s[k, 0:TILE, :] * _silu(pa_ref[ntile + k, SUBLANES:SUBLANES + TILE, :]))
    _keep_tail(pa_ref)


def _shift_rows(c, k, row):
    n = c.shape[0]
    if k % SUBLANES == 0:
        return jnp.concatenate([jnp.zeros((k, c.shape[1]), c.dtype), c[:n - k]], axis=0)
    return jnp.where(row >= k, pltpu.roll(c, k, 0), 0.0)


def _mixer_s5(pb_ref, ycat_ref, u_s, y_s, carry_ref, toep_ref, eb_ref, f_ref, pw_ref, d_ref, gw_ref, gb_ref, tick):
    for j in range(S5_SLABS):
        for sb in range(S5_J):
            u_s[j, sb * S5_PITCH:sb * S5_PITCH + S5_SPAN, :] = pb_ref[sb * S5_SPAN:(sb + 1) * S5_SPAN, j * LANES:(j + 1) * LANES]
    J = S5_J
    row = _row_iota((J, S5_SW))
    span = S5_PITCH
    for j in range(S5_SLABS):
        lhs = jnp.concatenate(
            [jnp.concatenate([u_s[j, pl.ds(S5_R * ph + t, J, stride=span), :] for t in range(S5_R)], axis=1)
             for ph in range(S5_PH)], axis=0)
        lhs = _bf(lhs)
        y_loc = _dot(lhs, toep_ref[j])
        e = _dot(lhs, eb_ref[j])
        tick()

        def pw(i):
            return pw_ref[j, 0, i:i + 1, :], pw_ref[j, 1, i:i + 1, :]

        l1r, l1i = pw(PW_L1)
        lr_ = [e[0:J, 0:S5_SW]]
        li_ = [e[0:J, S5_SW:2 * S5_SW]]
        for ph in range(1, S5_PH):
            mr, mi = _cmul(lr_[-1], li_[-1], l1r, l1i)
            lr_.append(mr + e[ph * J:(ph + 1) * J, 0:S5_SW])
            li_.append(mi + e[ph * J:(ph + 1) * J, S5_SW:2 * S5_SW])
        cr, ci = lr_[-1], li_[-1]
        for idx, k in enumerate(S5_HS_STEPS):
            qr, qi = pw(PW_HS + idx)
            mr, mi = _cmul(_shift_rows(cr, k, row), _shift_rows(ci, k, row), qr, qi)
            cr, ci = cr + mr, ci + mi
        cin_r = carry_ref[j, 0:1, :]
        cin_i = carry_ref[j, 1:2, :]
        tr = pw_ref[j, 0, PW_CARRY:PW_CARRY + J, :]
        ti = pw_ref[j, 1, PW_CARRY:PW_CARRY + J, :]
        mr, mi = _cmul(tr, ti, cin_r, cin_i)
        cr, ci = cr + mr, ci + mi
        carry_ref[j, 0:1, :] = cr[J - 1:J, :]
        carry_ref[j, 1:2, :] = ci[J - 1:J, :]
        cpr = jnp.where(row >= 1, pltpu.roll(cr, 1, 0), cin_r)
        cpi = jnp.where(row >= 1, pltpu.roll(ci, 1, 0), cin_i)
        sp_r, sp_i = [cpr], [cpi]
        for ph in range(1, S5_PH):
            qr, qi = pw(PW_L1 + ph - 1)
            mr, mi = _cmul(cpr, cpi, qr, qi)
            sp_r.append(lr_[ph - 1] + mr)
            sp_i.append(li_[ph - 1] + mi)
        sprev = jnp.concatenate([jnp.concatenate(sp_r, axis=0), jnp.concatenate(sp_i, axis=0)], axis=1)
        y = y_loc + _dot(_bf(sprev), f_ref[j])
        for ph in range(S5_PH):
            for t in range(S5_R):
                y_s[j, pl.ds(S5_R * ph + t, J, stride=span), :] = y[ph * J:(ph + 1) * J, t * LANES:(t + 1) * LANES]
    tick(2)
    yb = jnp.concatenate(
        [jnp.concatenate([y_s[j, sb * S5_PITCH:sb * S5_PITCH + S5_SPAN, :] for sb in range(S5_J)], axis=0)
         for j in range(S5_SLABS)], axis=1) + d_ref[...] * pb_ref[:, 0:D_BRANCH]
    yb = _gelu_tanh(yb)
    glu = _dot(_bf(yb), gw_ref[...])
    tick(2)
    yb = yb * _sigmoid(glu + gb_ref[...])
    ycat_ref[:, D_BRANCH:2 * D_BRANCH] = _bf(yb * _silu(pb_ref[:, D_BRANCH:2 * D_BRANCH]))


def _mixer_gla(pc_ref, ycat_ref, qd_s, ki_s, ke_s, gl_s, o_s, st_s, wg_ref, bg_ref, nw_ref, tick):
    k = pc_ref[:, GLA_DK:2 * GLA_DK]
    z_off = 2 * GLA_DK + D_BRANCH
    glow = pc_ref[:, 2 * GLA_DK + 2 * D_BRANCH:SEG_C_W]
    logits = _dot(_bf(glow), wg_ref[...]) + bg_ref[...]
    tick()
    g = -_softplus(-logits) / GLA_TAU
    gc = _chunk_cumsum(g)
    tick(2)
    qd_s[...] = _bf(pc_ref[:, 0:GLA_DK] * (GLA_HK ** -0.5) * jnp.exp(gc))
    ki_s[...] = _bf(k * jnp.exp(-gc))
    gl = jnp.concatenate(
        [jnp.broadcast_to(gc[c * CHUNK + CHUNK - 1:c * CHUNK + CHUNK, :], (CHUNK, GLA_DK)) for c in range(NCHUNK)], axis=0)
    ke_s[...] = _bf(k * jnp.exp(gl - gc))
    for c in range(NCHUNK):
        gl_s[c * SUBLANES:(c + 1) * SUBLANES, :] = jnp.exp(gl[c * CHUNK:c * CHUNK + SUBLANES, :])

    lane = _lane_iota((1, LANES))
    m_lo = (lane < GLA_HK).astype(BF16)
    m_hi = (lane >= GLA_HK).astype(BF16)
    tri = (_lane_iota((CHUNK, CHUNK)) <= _row_iota((CHUNK, CHUNK)))

    for c in range(NCHUNK):
        if c % 2 == 0:
            tick()
        r0 = c * CHUNK
        rows = pl.ds(r0, CHUNK)
        for tl in range(GLA_DK // LANES):
            cols = slice(tl * LANES, (tl + 1) * LANES)
            qd = qd_s[rows, cols]
            ki = ki_s[rows, cols]
            ke = ke_s[rows, cols]
            dec = gl_s[pl.ds(c * SUBLANES, 1), cols]
            qstack = jnp.concatenate([qd * m_lo, qd * m_hi], axis=0)
            s2 = _dot_nt(qstack, ki)
            for hh in range(2):
                h = 2 * tl + hh
                vh = _bf(pc_ref[rows, 2 * GLA_DK + h * GLA_HV:2 * GLA_DK + (h + 1) * GLA_HV])
                sc = jnp.where(tri, s2[hh * CHUNK:(hh + 1) * CHUNK, :], 0.0)
                qh = qstack[hh * CHUNK:(hh + 1) * CHUNK, :]
                st = st_s[h]
                o = _dot(_bf(sc), vh) + _dot_nt(qh, _bf(st))
                o_s[h, rows, :] = o
                st_s[h] = st * dec + _dot_tn(vh, ke)

    tick()
    for h in range(GLA_HEADS):
        o = o_s[h, 0:TILE, :]
        o = o * lax.rsqrt(jnp.mean(o * o, axis=-1, keepdims=True) + EPS) * nw_ref[...]
        zh = pc_ref[:, z_off + h * GLA_HV:z_off + (h + 1) * GLA_HV]
        ycat_ref[:, 2 * D_BRANCH + h * GLA_HV:2 * D_BRANCH + (h + 1) * GLA_HV] = _bf(o * _silu(zh))


def _mixer_ssd(pd_ref, ycat_ref, xbc_s, ac_s, dt_s, y_s, st_s,
               cw_ref, cb_ref, dtb_ref, alog_ref, dsk_ref, nw_ref, tick):
    zt = D_BRANCH // LANES
    xt = 2 * D_BRANCH // LANES
    body = slice(SUBLANES, SUBLANES + TILE)
    tick(2)
    for k in range(xt):
        for r, ur in enumerate(_phase_conv(pd_ref, zt + k, cw_ref, cb_ref, k * LANES)):
            xbc_s[k, pl.ds(r, NPH, stride=CONV_PH), :] = _silu(ur)
    dt = _softplus(pd_ref[zt + xt, body, :] + dtb_ref[...])
    a = -jnp.exp(alog_ref[...])
    dt_s[...] = dt
    ac_s[...] = _chunk_cumsum(dt * a)
    tick()

    lane = _lane_iota((1, LANES))
    lo = lane < SSD_HDIM
    m_lo = lo.astype(F32)
    m_hi = 1.0 - m_lo
    lane_j = jnp.where(lo, lane, lane - SSD_HDIM)
    tri2 = lane_j <= _row_iota((CHUNK, LANES))
    npairs = SSD_HEADS // 2
    ppg = npairs // SSD_GROUPS

    for c in range(NCHUNK):
        tick()
        r0 = c * CHUNK
        rows = pl.ds(r0, CHUNK)
        ac = ac_s[rows, :]
        dtc = dt_s[rows, :]
        a_last = ac_s[pl.ds(r0 + CHUNK - 1, 1), :]
        dend = jnp.exp(a_last - ac) * dtc
        cdec = jnp.exp(a_last)
        act = jnp.concatenate([ac, ac], axis=0).T
        dtt = jnp.concatenate([dtc, dtc], axis=0).T
        for g in range(SSD_GROUPS):
            bm = xbc_s[npairs + g, rows, :]
            cm = xbc_s[npairs + SSD_GROUPS + g, rows, :]
            bmb = _bf(bm)
            cb2 = _dot_nt(_bf(cm), jnp.concatenate([bmb, bmb], axis=0))
            st = st_s[g]
            y_off = _dot(_bf(cm), _bf(st))
            xd, cdrow = [], []
            for pp in range(ppg):
                pr = g * ppg + pp
                h0, h1 = 2 * pr, 2 * pr + 1
                xp = xbc_s[pr, rows, :]
                acol = jnp.where(lo, jnp.broadcast_to(ac[:, h0:h0 + 1], (CHUNK, LANES)),
                                 jnp.broadcast_to(ac[:, h1:h1 + 1], (CHUNK, LANES)))
                arow = jnp.where(lo, act[h0:h0 + 1, :], act[h1:h1 + 1, :])
                dtrow = jnp.where(lo, dtt[h0:h0 + 1, :], dtt[h1:h1 + 1, :])
                seg = jnp.where(tri2, jnp.exp(jnp.where(tri2, acol - arow, 0.0)), 0.0)
                m = cb2 * seg * dtrow
                xbd = jnp.concatenate([xp * m_lo, xp * m_hi], axis=0)
                y = _dot(_bf(m), _bf(xbd))
                y = y + y_off[:, pp * LANES:(pp + 1) * LANES] * jnp.exp(acol)
                y = y + dsk_ref[:, pr * LANES:(pr + 1) * LANES] * xp
                y_s[pr, rows, :] = y
                dcol = jnp.where(lo, jnp.broadcast_to(dend[:, h0:h0 + 1], (CHUNK, LANES)),
                                 jnp.broadcast_to(dend[:, h1:h1 + 1], (CHUNK, LANES)))
                xd.append(xp * dcol)
                cdrow.append(jnp.where(lo, jnp.broadcast_to(cdec[:, h0:h0 + 1], (1, LANES)),
                                       jnp.broadcast_to(cdec[:, h1:h1 + 1], (1, LANES))))
            snew = _dot_tn(bmb, _bf(jnp.concatenate(xd, axis=1)))
            st_s[g] = st * jnp.concatenate(cdrow, axis=1) + snew

    z = jnp.concatenate([pd_ref[k, body, :] for k in range(zt)], axis=1)
    y = jnp.concatenate([y_s[pr, 0:TILE, :] for pr in range(npairs)], axis=1) * _silu(z)
    _keep_tail(pd_ref)
    y = y * lax.rsqrt(jnp.mean(y * y, axis=-1, keepdims=True) + EPS) * nw_ref[...]
    ycat_ref[:, 3 * D_BRANCH:4 * D_BRANCH] = _bf(y)


class _Interleave:
    def __init__(self):
        self.queue = []

    def load(self, thunks):
        self.flush()
        self.queue = list(thunks)

    def tick(self, n=1):
        for _ in range(n):
            if self.queue:
                self.queue.pop(0)()

    def flush(self):
        while self.queue:
            self.queue.pop(0)()


PROJ_CHUNK = 256


def _layer_kernel(final, tiles_per_seq,
                  x_ref, nw_ref, win_ref, wind_ref,
                  a_cw, a_cb, a_wg, a_bg, a_l,
                  b_toep, b_eb, b_f, b_pw, b_d, b_gw, b_gb,
                  c_wg, c_bg, c_nw,
                  d_cw, d_cb, d_dtb, d_alog, d_dsk, d_nw,
                  wout_ref, nf_ref,
                  o_ref,
                  pa_s, pb_s, pc_s, pd_s, xs_s, hb_s,
                  ycat, buf0, buf1, lru_carry, s5_carry,
                  qd_s, ki_s, ke_s, gl_s, gla_st, xbc_s, ac_s, dt_s, ssd_st):
    g = pl.program_id(0)

    @pl.when(g == 0)
    def _():
        pa_s[...] = jnp.zeros_like(pa_s)
        pb_s[...] = jnp.zeros_like(pb_s)
        pc_s[...] = jnp.zeros_like(pc_s)
        pd_s[...] = jnp.zeros_like(pd_s)
        xs_s[...] = jnp.zeros_like(xs_s)
        hb_s[...] = jnp.zeros_like(hb_s)

    @pl.when((g == 0) | (lax.rem(g - 1, tiles_per_seq) == 0))
    def _():
        pa_s[:, 0:SUBLANES, :] = jnp.zeros((pa_s.shape[0], SUBLANES, LANES), F32)
        pd_s[:, 0:SUBLANES, :] = jnp.zeros((pd_s.shape[0], SUBLANES, LANES), F32)
        lru_carry[...] = jnp.zeros_like(lru_carry)
        s5_carry[...] = jnp.zeros_like(s5_carry)
        gla_st[...] = jnp.zeros_like(gla_st)
        ssd_st[...] = jnp.zeros_like(ssd_st)

    def proj_chunks(w_ref, w_off, dst_ref, width, tiled):
        out = []
        for c0 in range(0, width, PROJ_CHUNK):
            c1 = min(c0 + PROJ_CHUNK, width)

            def thunk(c0=c0, c1=c1):
                res = _dot(hb_s[...], w_ref[:, w_off + c0:w_off + c1])
                if tiled:
                    for i in range((c1 - c0) // LANES):
                        dst_ref[c0 // LANES + i, SUBLANES:SUBLANES + TILE, :] = res[:, i * LANES:(i + 1) * LANES]
                else:
                    dst_ref[:, c0:c1] = res
            out.append(thunk)
        return out

    def out_chunks(m):
        out = []
        rows = slice(m * D_BRANCH, (m + 1) * D_BRANCH)
        for c0 in range(0, D_MODEL, PROJ_CHUNK):
            cols = slice(c0, c0 + PROJ_CHUNK)

            def thunk(cols=cols):
                base = xs_s[:, cols] if m == 0 else o_ref[:, cols]
                o_ref[:, cols] = base + _dot(ycat[:, rows], wout_ref[rows, cols])
            out.append(thunk)
        return out

    il = _Interleave()
    il.load(proj_chunks(wind_ref, 0, pd_s, SEG_D_W, True))
    ntile = D_BRANCH // LANES
    p2_s, l2_s = _SubScratch(xbc_s, 0, 0), _SubScratch(xbc_s, ntile, 0)
    c2_s = _SubScratch(xbc_s, 0, TILE // 2)
    _mixer_lru(pa_s, ycat, buf0, buf1, p2_s, l2_s, c2_s, lru_carry, a_cw, a_cb, a_wg, a_bg, a_l, il.tick)
    il.flush()
    hb_s[...] = _bf(_rmsnorm(x_ref[...], nw_ref[...]))
    il.load(proj_chunks(win_ref, SEG_A, pa_s, SEG_B - SEG_A, True) + out_chunks(0))
    _mixer_s5(pb_s, ycat, buf0, buf1, s5_carry, b_toep, b_eb, b_f, b_pw, b_d, b_gw, b_gb, il.tick)
    il.load(proj_chunks(win_ref, SEG_B, pb_s, SEG_C - SEG_B, False) + out_chunks(1))
    _mixer_gla(pc_s, ycat, qd_s, ki_s, ke_s, gl_s, buf1, gla_st, c_wg, c_bg, c_nw, il.tick)
    il.load(proj_chunks(win_ref, SEG_C, pc_s, SEG_C_W, False) + out_chunks(2))
    _mixer_ssd(pd_s, ycat, xbc_s, ac_s, dt_s, buf1, ssd_st, d_cw, d_cb, d_dtb, d_alog, d_dsk, d_nw, il.tick)
    il.load(out_chunks(3))
    il.flush()
    if final:
        o_ref[...] = _rmsnorm(o_ref[...], nf_ref[...])
    xs_s[...] = x_ref[...]


def _layer_call(layer, final, x, consts):
    bsz, seq, _ = x.shape
    assert seq % TILE == 0
    tiles_per_seq = seq // TILE
    ntiles = bsz * tiles_per_seq

    def wspec(arr):
        nd = arr.ndim - 1
        return pl.BlockSpec((None,) + arr.shape[1:], lambda g, _n=nd: (layer,) + (0,) * _n,
                            pipeline_mode=pl.Buffered(1))

    def cspec(arr):
        nd = arr.ndim
        return pl.BlockSpec(arr.shape, lambda g, _n=nd: (0,) * _n, pipeline_mode=pl.Buffered(1))

    def tile_index(t):
        return (t // tiles_per_seq, t % tiles_per_seq, 0)

    per_layer = consts["per_layer"]
    shared = consts["shared"]
    in_x = pl.BlockSpec((None, TILE, D_MODEL), lambda g: tile_index(jnp.minimum(g, ntiles - 1)))
    out_x = pl.BlockSpec((None, TILE, D_MODEL), lambda g: tile_index(jnp.maximum(g - 1, 0)))
    scratch = [
        pltpu.VMEM(((SEG_B - SEG_A) // LANES, SUBLANES + TILE, LANES), F32),
        pltpu.VMEM((TILE, SEG_C - SEG_B), F32),
        pltpu.VMEM((TILE, SEG_C_W), F32),
        pltpu.VMEM((SEG_D_W // LANES, SUBLANES + TILE, LANES), F32),
        pltpu.VMEM((TILE, D_MODEL), F32),
        pltpu.VMEM((TILE, D_MODEL), BF16),
        pltpu.VMEM((TILE, 4 * D_BRANCH), BF16),
        pltpu.VMEM((4, S5_ROWS, LANES), F32),
        pltpu.VMEM((4, S5_ROWS, LANES), F32),
        pltpu.VMEM((SUBLANES, D_BRANCH), F32),
        pltpu.VMEM((S5_SLABS, SUBLANES, S5_SW), F32),
        pltpu.VMEM((TILE, GLA_DK), BF16),
        pltpu.VMEM((TILE, GLA_DK), BF16),
        pltpu.VMEM((TILE, GLA_DK), BF16),
        pltpu.VMEM((NCHUNK * SUBLANES, GLA_DK), F32),
        pltpu.VMEM((GLA_HEADS, GLA_HV, LANES), F32),
        pltpu.VMEM((2 * D_BRANCH // LANES, TILE, LANES), F32),
        pltpu.VMEM((TILE, LANES), F32),
        pltpu.VMEM((TILE, LANES), F32),
        pltpu.VMEM((SSD_GROUPS, SSD_STATE, D_BRANCH // SSD_GROUPS), F32),
    ]
    return pl.pallas_call(
        functools.partial(_layer_kernel, final, tiles_per_seq),
        grid=(ntiles + 1,),
        in_specs=[in_x] + [wspec(a) for a in per_layer] + [cspec(a) for a in shared],
        out_specs=out_x,
        out_shape=jax.ShapeDtypeStruct(x.shape, F32),
        scratch_shapes=scratch,
        compiler_params=pltpu.CompilerParams(
            dimension_semantics=("arbitrary",),
            vmem_limit_bytes=VMEM_LIMIT_BYTES),
        name=f"layer{layer}",
    )(x, *per_layer, *shared)


def _prepare(norm_w, w_in, lru_conv_w, lru_conv_b, lru_w_r, lru_b_r, lru_w_i, lru_b_i, lru_l,
             s5_lam_re, s5_lam_im, s5_log_dt, s5_b_re, s5_b_im, s5_c_re, s5_c_im, s5_d, s5_glu_w, s5_glu_b,
             gla_w_gate, gla_b_gate, gla_norm_w,
             ssd_conv_w, ssd_conv_b, ssd_dt_bias, ssd_a_log, ssd_d, ssd_norm_w,
             w_out, norm_f_w):
    nl = w_in.shape[0]
    row = lambda a: a.reshape(nl, 1, -1).astype(F32)
    w_abc = jnp.pad(w_in[:, :, :ORIG_C_END], ((0, 0), (0, 0), (0, SEG_D - ORIG_C_END))).astype(BF16)
    w_d = jnp.pad(w_in[:, :, ORIG_C_END:], ((0, 0), (0, 0), (0, SEG_D_W - (ORIG_D_IN - ORIG_C_END)))).astype(BF16)
    hph = LRU_HEADS // 2
    eye = jnp.eye(hph, dtype=F32)

    def bd(w):
        w = w.reshape(nl, 2, hph, LRU_HDIM, LRU_HDIM)
        return jnp.einsum("lfhij,hk->lfhikj", w, eye).reshape(nl, 2, hph * LRU_HDIM, hph * LRU_HDIM)

    a_wg = jnp.concatenate([bd(lru_w_r), bd(lru_w_i)], axis=-1).astype(BF16)
    half = D_BRANCH // 2
    br = lru_b_r.reshape(nl, 2, half)
    bi = lru_b_i.reshape(nl, 2, half)
    a_bg = jnp.concatenate([br, bi], axis=-1).reshape(nl, 1, 2 * D_BRANCH)
    toep, ebm, fm, pw = _s5_prepare(s5_lam_re, s5_lam_im, s5_log_dt, s5_b_re, s5_b_im, s5_c_re, s5_c_im)
    c_wg = jnp.concatenate([gla_w_gate, jnp.zeros((nl, LANES - GLA_RANK, GLA_DK), gla_w_gate.dtype)], axis=1).astype(BF16)
    pad_h = lambda a: jnp.concatenate([a, jnp.zeros((nl, LANES - SSD_HEADS), a.dtype)], axis=-1).reshape(nl, 1, LANES)
    per_layer = [
        row(norm_w), w_abc, w_d,
        lru_conv_w.astype(F32), row(lru_conv_b), a_wg, a_bg, row(lru_l),
        toep, ebm, fm, pw, row(s5_d), s5_glu_w.astype(BF16), row(s5_glu_b),
        c_wg, row(gla_b_gate), row(gla_norm_w),
        ssd_conv_w.astype(F32), row(ssd_conv_b), pad_h(ssd_dt_bias), pad_h(ssd_a_log),
        row(jnp.repeat(ssd_d, SSD_HDIM, axis=-1)), row(ssd_norm_w),
        w_out.astype(BF16),
    ]
    shared = [norm_f_w.reshape(1, D_MODEL).astype(F32)]
    return {"per_layer": per_layer, "shared": shared}


def kernel(x, norm_w, w_in, lru_conv_w, lru_conv_b, lru_w_r, lru_b_r, lru_w_i, lru_b_i, lru_l, s5_lam_re, s5_lam_im, s5_log_dt, s5_b_re, s5_b_im, s5_c_re, s5_c_im, s5_d, s5_glu_w, s5_glu_b, gla_w_gate, gla_b_gate, gla_norm_w, ssd_conv_w, ssd_conv_b, ssd_dt_bias, ssd_a_log, ssd_d, ssd_norm_w, w_out, norm_f_w):
    consts = _prepare(norm_w, w_in, lru_conv_w, lru_conv_b, lru_w_r, lru_b_r, lru_w_i, lru_b_i, lru_l,
                      s5_lam_re, s5_lam_im, s5_log_dt, s5_b_re, s5_b_im, s5_c_re, s5_c_im, s5_d, s5_glu_w, s5_glu_b,
                      gla_w_gate, gla_b_gate, gla_norm_w,
                      ssd_conv_w, ssd_conv_b, ssd_dt_bias, ssd_a_log, ssd_d, ssd_norm_w,
                      w_out, norm_f_w)
    nl = w_in.shape[0]
    for layer in range(nl):
        x = _layer_call(layer, layer == nl - 1, x, consts)
    return x
```

```python
import functools
import math

import jax
import jax.numpy as jnp
from jax import lax
from jax.experimental import pallas as pl
from jax.experimental.pallas import tpu as pltpu

F32 = jnp.float32
BF16 = jnp.bfloat16

D_MODEL = 1024
D_BRANCH = 512
CONV_W = 4
EPS = 1e-6
LRU_HEADS = 8
LRU_HDIM = 64
LRU_C = 8.0
S5_GROUP = 16
S5_GROUPS = 32
S5_STATE = 64
GLA_HEADS = 4
GLA_DK = 256
GLA_HK = 64
GLA_HV = 128
GLA_RANK = 16
GLA_TAU = 16.0
SSD_HEADS = 8
SSD_HDIM = 64
SSD_GROUPS = 2
SSD_STATE = 128
CHUNK = 64

LANES = 128
SUBLANES = 8
VMEM_LIMIT_BYTES = 63 * 1024 * 1024

TILE = 512
NCHUNK = TILE // CHUNK
S5_R = 4
S5_PH = 8
S5_NB = TILE // S5_R
S5_J = S5_NB // S5_PH
S5_SLABS = D_BRANCH // LANES
S5_GPS = LANES // S5_GROUP
S5_SW = S5_GPS * S5_STATE
S5_SPAN = S5_PH * S5_R
S5_PITCH = S5_SPAN + SUBLANES
S5_ROWS = S5_J * S5_PITCH
S5_HS_STEPS = tuple(k for k in (1, 2, 4, 8, 16, 32) if k < S5_J)
PW_L1 = 0
PW_HS = S5_PH - 1
PW_CARRY = PW_HS + len(S5_HS_STEPS)
PW_ROWS = PW_CARRY + S5_J

SEG_A = 0
SEG_B = 1024
SEG_C = 2048
SEG_C_W = 1664
SEG_D = SEG_C + SEG_C_W
SEG_D_W = 1664
N_IN = SEG_D + SEG_D_W
ORIG_C_END = 3600
ORIG_D_IN = 5144


def _dot(a, b):
    return jnp.dot(a, b, preferred_element_type=F32)


def _dot_nt(a, b):
    return lax.dot_general(a, b, (((1,), (1,)), ((), ())), preferred_element_type=F32)


def _dot_tn(a, b):
    return lax.dot_general(a, b, (((0,), (0,)), ((), ())), preferred_element_type=F32)


def _bf(x):
    return x.astype(BF16)


def _softplus(x):
    return jnp.maximum(x, 0.0) + jnp.log1p(jnp.exp(-jnp.abs(x)))


def _sigmoid(x):
    return 0.5 * jnp.tanh(0.5 * x) + 0.5


def _silu(x):
    return x * _sigmoid(x)


def _gelu_tanh(x):
    c = math.sqrt(2.0 / math.pi)
    return 0.5 * x * (1.0 + jnp.tanh(c * (x + 0.044715 * (x * x * x))))


def _cmul(ar, ai, br, bi):
    return ar * br - ai * bi, ar * bi + ai * br


def _split_hi_lo(x):
    hi = x.astype(BF16)
    lo = (x - hi.astype(F32)).astype(BF16)
    return hi, lo


def _chunk_cumsum(x):
    hi, lo = _split_hi_lo(x)
    lo2 = (x - hi.astype(F32) - lo.astype(F32)).astype(BF16)
    ri = lax.broadcasted_iota(jnp.int32, (CHUNK, CHUNK), 0)
    ci = lax.broadcasted_iota(jnp.int32, (CHUNK, CHUNK), 1)
    t = (ci <= ri).astype(BF16)
    out = []
    for c in range(x.shape[0] // CHUNK):
        rows = slice(c * CHUNK, (c + 1) * CHUNK)
        out.append(_dot(t, hi[rows]) + _dot(t, lo[rows]) + _dot(t, lo2[rows]))
    return jnp.concatenate(out, axis=0)


def _s5_prep_kernel(lr_ref, li_ref, ldt_ref, lrc_ref, lic_ref, bt_re_ref, bt_im_ref, c_re_ref, c_im_ref,
                    ct_re_ref, ct_im_ref, toep_ref, eb_ref, f_ref, pw_ref):
    def discretise(lr, li, dt):
        mag = jnp.exp(lr * dt)
        ab_re = mag * jnp.cos(li * dt)
        ab_im = mag * jnp.sin(li * dt)
        den = lr * lr + li * li
        nr = ab_re - 1.0
        coef_re = (nr * lr + ab_im * li) / den
        coef_im = (ab_im * lr - nr * li) / den
        return ab_re, ab_im, coef_re, coef_im

    def powers(ab_re, ab_im, n):
        out = [(jnp.ones_like(ab_re), jnp.zeros_like(ab_re)), (ab_re, ab_im)]
        for _ in range(2, n + 1):
            out.append(_cmul(out[-1][0], out[-1][1], ab_re, ab_im))
        return out

    ab_re, ab_im, cf_re, cf_im = discretise(lr_ref[...], li_ref[...], jnp.exp(ldt_ref[...]))
    low = powers(ab_re, ab_im, S5_R)
    l1 = low[S5_R]
    l1_pows = [l1]
    for _ in range(2, S5_PH + 1):
        l1_pows.append(_cmul(l1_pows[-1][0], l1_pows[-1][1], l1[0], l1[1]))
    l2 = l1_pows[S5_PH - 1]
    for r in range(1, S5_PH):
        pw_ref[0, PW_L1 + r - 1] = l1_pows[r - 1][0]
        pw_ref[1, PW_L1 + r - 1] = l1_pows[r - 1][1]
    sq = l2
    for i, k in enumerate(S5_HS_STEPS):
        pw_ref[0, PW_HS + i] = sq[0]
        pw_ref[1, PW_HS + i] = sq[1]
        sq = _cmul(sq[0], sq[1], sq[0], sq[1])
    cur = l2
    for j in range(S5_J):
        pw_ref[0, PW_CARRY + j] = cur[0]
        pw_ref[1, PW_CARRY + j] = cur[1]
        cur = _cmul(cur[0], cur[1], l2[0], l2[1])

    toep_ref[...] = jnp.zeros_like(toep_ref)
    eb_ref[...] = jnp.zeros_like(eb_ref)
    f_ref[...] = jnp.zeros_like(f_ref)
    hi = lax.Precision.HIGHEST
    for g in range(S5_GPS):
        row_pw = [(p[0][g:g + 1, :], p[1][g:g + 1, :]) for p in low]
        bbt_re, bbt_im = _cmul(cf_re[g:g + 1, :], cf_im[g:g + 1, :], bt_re_ref[g], bt_im_ref[g])
        for t in range(S5_R):
            e_re, e_im = _cmul(row_pw[S5_R - 1 - t][0], row_pw[S5_R - 1 - t][1], bbt_re, bbt_im)
            rows = slice(t * LANES + g * S5_GROUP, t * LANES + (g + 1) * S5_GROUP)
            eb_ref[rows, g * S5_STATE:(g + 1) * S5_STATE] = _bf(e_re)
            eb_ref[rows, S5_SW + g * S5_STATE:S5_SW + (g + 1) * S5_STATE] = _bf(e_im)
        for k in range(S5_R):
            cl_re, cl_im = _cmul(c_re_ref[g], c_im_ref[g], row_pw[k][0], row_pw[k][1])
            kt = (lax.dot_general(bbt_re, cl_re, (((1,), (1,)), ((), ())), preferred_element_type=F32, precision=hi)
                  - lax.dot_general(bbt_im, cl_im, (((1,), (1,)), ((), ())), preferred_element_type=F32, precision=hi))
            ktb = _bf(kt)
            for t in range(S5_R - k):
                r = t + k
                toep_ref[t * LANES + g * S5_GROUP:t * LANES + (g + 1) * S5_GROUP,
                         r * LANES + g * S5_GROUP:r * LANES + (g + 1) * S5_GROUP] = ktb
        abc_re, abc_im, _, _ = discretise(lrc_ref[g], lic_ref[g], jnp.exp(ldt_ref[g:g + 1, :]))
        col_pw = powers(abc_re, abc_im, S5_R)
        for r in range(S5_R):
            f_re, f_im = _cmul(ct_re_ref[g], ct_im_ref[g], col_pw[r + 1][0], col_pw[r + 1][1])
            cols = slice(r * LANES + g * S5_GROUP, r * LANES + (g + 1) * S5_GROUP)
            f_ref[g * S5_STATE:(g + 1) * S5_STATE, cols] = _bf(f_re)
            f_ref[S5_SW + g * S5_STATE:S5_SW + (g + 1) * S5_STATE, cols] = _bf(-f_im)


def _s5_prepare(lam_re, lam_im, log_dt, b_re, b_im, c_re, c_im):
    nl = lam_re.shape[0]
    g8 = S5_GPS
    rl = S5_R * LANES

    def im4(l, j):
        return (l, j, 0, 0)

    def im3(l, j):
        return (l, j, 0)

    gsn = pl.BlockSpec((None, g8, S5_GROUP, S5_STATE), im4)
    toep, ebm, fm, pw = pl.pallas_call(
        _s5_prep_kernel,
        grid=(nl, S5_SLABS),
        in_specs=[
            pl.BlockSpec((None, g8, S5_STATE), im3),
            pl.BlockSpec((None, g8, S5_STATE), im3),
            pl.BlockSpec((None, g8, 1), im3),
            pl.BlockSpec((None, g8, S5_STATE, 1), im4),
            pl.BlockSpec((None, g8, S5_STATE, 1), im4),
            gsn, gsn, gsn, gsn,
            pl.BlockSpec((None, g8, S5_STATE, S5_GROUP), im4),
            pl.BlockSpec((None, g8, S5_STATE, S5_GROUP), im4),
        ],
        out_specs=[
            pl.BlockSpec((None, None, rl, rl), im4),
            pl.BlockSpec((None, None, rl, 2 * S5_SW), im4),
            pl.BlockSpec((None, None, 2 * S5_SW, rl), im4),
            pl.BlockSpec((None, None, 2, PW_ROWS, g8, S5_STATE), lambda l, j: (l, j, 0, 0, 0, 0)),
        ],
        out_shape=[
            jax.ShapeDtypeStruct((nl, S5_SLABS, rl, rl), BF16),
            jax.ShapeDtypeStruct((nl, S5_SLABS, rl, 2 * S5_SW), BF16),
            jax.ShapeDtypeStruct((nl, S5_SLABS, 2 * S5_SW, rl), BF16),
            jax.ShapeDtypeStruct((nl, S5_SLABS, 2, PW_ROWS, g8, S5_STATE), F32),
        ],
        name="s5_prepare",
    )(lam_re, lam_im, log_dt[..., None], lam_re[..., None], lam_im[..., None],
      jnp.swapaxes(b_re, -1, -2), jnp.swapaxes(b_im, -1, -2), c_re, c_im,
      jnp.swapaxes(c_re, -1, -2), jnp.swapaxes(c_im, -1, -2))
    return toep, ebm, fm, pw.reshape(nl, S5_SLABS, 2, PW_ROWS, S5_SW)


def _rmsnorm(x, w):
    return x * lax.rsqrt(jnp.mean(x * x, axis=-1, keepdims=True) + EPS) * w


CONV_PH = 4
NPH = TILE // CONV_PH


def _phase_conv(p_ref, tile, w_ref, b_ref, c0):
    cols = slice(c0, c0 + LANES)
    w = w_ref[:, cols]
    b = b_ref[:, cols]
    xs = {r: p_ref[tile, pl.ds(SUBLANES + r, NPH, stride=CONV_PH), :] for r in range(1 - CONV_W, CONV_PH)}
    out = []
    for r in range(CONV_PH):
        acc = b + w[CONV_W - 1:CONV_W, :] * xs[r]
        for k in range(CONV_W - 1):
            acc = acc + w[k:k + 1, :] * xs[r - (CONV_W - 1) + k]
        out.append(acc)
    return out


def _keep_tail(p_ref):
    p_ref[:, 0:SUBLANES, :] = p_ref[:, TILE:TILE + SUBLANES, :]


def _row_iota(shape):
    return lax.broadcasted_iota(jnp.int32, shape, 0)


def _lane_iota(shape):
    return lax.broadcasted_iota(jnp.int32, shape, 1)


SCAN_RADIX = 4
SCAN_SIZES = [TILE]
while SCAN_SIZES[-1] > SUBLANES:
    SCAN_SIZES.append(SCAN_SIZES[-1] // SCAN_RADIX)
assert SCAN_SIZES[-1] == SUBLANES
SCAN_SMALL_ROWS = sum(SCAN_SIZES[1:-1])


class _SubScratch:
    def __init__(self, ref, tile0, row0):
        self.ref, self.tile0, self.row0 = ref, tile0, row0

    @property
    def at(self):
        return self

    def __getitem__(self, idx):
        k, rows, lanes = idx
        return self.ref.at[self.tile0 + k, pl.ds(self.row0 + rows.start, rows.size, stride=rows.stride), lanes]


def _lru_scan(a_s, b_s, p2_s, l2_s, c2_s, carry_ref, tick):
    radix = SCAN_RADIX
    sizes = SCAN_SIZES
    nlev = len(sizes) - 1
    offs = [0, 0] + [sum(sizes[1:i]) for i in range(2, nlev)]
    row = _row_iota((SUBLANES, LANES))
    for k in range(D_BRANCH // LANES):
        tick()

        def phase(big, small, lev, r):
            ref = big if lev == 0 else small
            return ref.at[k, pl.ds(offs[lev] + r, sizes[lev + 1], stride=radix), :]

        def dense(small, lev):
            return small.at[k, pl.ds(offs[lev], sizes[lev]), :]

        for lev in range(nlev):
            p = phase(a_s, p2_s, lev, 0)[...]
            l = phase(b_s, l2_s, lev, 0)[...]
            for r in range(1, radix):
                ar = phase(a_s, p2_s, lev, r)[...]
                br = phase(b_s, l2_s, lev, r)[...]
                l = ar * l + br
                p = ar * p
                phase(a_s, p2_s, lev, r)[...] = p
                phase(b_s, l2_s, lev, r)[...] = l
            if lev + 1 < nlev:
                dense(p2_s, lev + 1)[...] = p
                dense(l2_s, lev + 1)[...] = l
        p3, l3 = p, l
        for sh in (1, 2, 4):
            psh = jnp.where(row >= sh, pltpu.roll(p3, sh, 0), 1.0)
            lsh = jnp.where(row >= sh, pltpu.roll(l3, sh, 0), 0.0)
            l3 = p3 * lsh + l3
            p3 = p3 * psh
        cin = carry_ref[:, k * LANES:(k + 1) * LANES]
        c3 = l3 + p3 * cin
        c3prev = jnp.where(row >= 1, pltpu.roll(c3, 1, 0), cin)
        carry_ref[:, k * LANES:(k + 1) * LANES] = jnp.broadcast_to(c3[SUBLANES - 1:SUBLANES, :], (SUBLANES, LANES))
        cprev = c3prev
        for lev in range(nlev - 1, 0, -1):
            phase(None, c2_s, lev, 0)[...] = cprev
            for r in range(1, radix):
                phase(None, c2_s, lev, r)[...] = (phase(b_s, l2_s, lev, r - 1)[...]
                                                  + phase(a_s, p2_s, lev, r - 1)[...] * cprev)
            cprev = dense(c2_s, lev)[...]
        for r in range(radix):
            phase(b_s, None, 0, r)[...] = phase(b_s, None, 0, r)[...] + phase(a_s, None, 0, r)[...] * cprev


def _mixer_lru(pa_ref, ycat_ref, a_s, b_s, p2_s, l2_s, c2_s, carry_ref,
               cw_ref, cb_ref, wg_ref, bg_ref, ll_ref, tick):
    tick()
    ntile = D_BRANCH // LANES
    for k in range(ntile):
        for r, ur in enumerate(_phase_conv(pa_ref, k, cw_ref, cb_ref, k * LANES)):
            a_s[k, pl.ds(r, NPH, stride=CONV_PH), :] = ur
    u = jnp.concatenate([a_s[k, 0:TILE, :] for k in range(ntile)], axis=1)
    ub = _bf(u)
    half = D_BRANCH // 2
    rs, is_ = [], []
    for hf in range(2):
        lg = _dot(ub[:, hf * half:(hf + 1) * half], wg_ref[hf]) + bg_ref[:, hf * D_BRANCH:(hf + 1) * D_BRANCH]
        tick()
        rs.append(_sigmoid(lg[:, 0:half]))
        is_.append(_sigmoid(lg[:, half:2 * half]))
    r = jnp.concatenate(rs, axis=1)
    i = jnp.concatenate(is_, axis=1)
    coef = -LRU_C * _softplus(-ll_ref[...])
    a = jnp.exp(coef * r)
    om = 1.0 - a * a
    mult = jnp.where(om > 0.0, om * lax.rsqrt(om), 0.0)
    b = mult * i * u
    for k in range(D_BRANCH // LANES):
        a_s[k, 0:TILE, :] = a[:, k * LANES:(k + 1) * LANES]
        b_s[k, 0:TILE, :] = b[:, k * LANES:(k + 1) * LANES]
    _lru_scan(a_s, b_s, p2_s, l2_s, c2_s, carry_ref, tick)
    for k in range(D_BRANCH // LANES):
        ycat_ref[:, k * LANES:(k + 1) * LANES] = _bf(
            b_s[k, 0:TILE, :] * _silu(pa_ref[ntile + k, SUBLANES:SUBLANES + TILE, :]))
    _keep_tail(pa_ref)


def _shift_rows(c, k, row):
    n = c.shape[0]
    if k % SUBLANES == 0:
        return jnp.concatenate([jnp.zeros((k, c.shape[1]), c.dtype), c[:n - k]], axis=0)
    return jnp.where(row >= k, pltpu.roll(c, k, 0), 0.0)


def _mixer_s5(pb_ref, ycat_ref, u_s, y_s, carry_ref, toep_ref, eb_ref, f_ref, pw_ref, d_ref, gw_ref, gb_ref, tick):
    for j in range(S5_SLABS):
        for sb in range(S5_J):
            u_s[j, sb * S5_PITCH:sb * S5_PITCH + S5_SPAN, :] = pb_ref[sb * S5_SPAN:(sb + 1) * S5_SPAN, j * LANES:(j + 1) * LANES]
    J = S5_J
    row = _row_iota((J, S5_SW))
    span = S5_PITCH
    for j in range(S5_SLABS):
        lhs = jnp.concatenate(
            [jnp.concatenate([u_s[j, pl.ds(S5_R * ph + t, J, stride=span), :] for t in range(S5_R)], axis=1)
             for ph in range(S5_PH)], axis=0)
        lhs = _bf(lhs)
        y_loc = _dot(lhs, toep_ref[j])
        e = _dot(lhs, eb_ref[j])
        tick()

        def pw(i):
            return pw_ref[j, 0, i:i + 1, :], pw_ref[j, 1, i:i + 1, :]

        l1r, l1i = pw(PW_L1)
        lr_ = [e[0:J, 0:S5_SW]]
        li_ = [e[0:J, S5_SW:2 * S5_SW]]
        for ph in range(1, S5_PH):
            mr, mi = _cmul(lr_[-1], li_[-1], l1r, l1i)
            lr_.append(mr + e[ph * J:(ph + 1) * J, 0:S5_SW])
            li_.append(mi + e[ph * J:(ph + 1) * J, S5_SW:2 * S5_SW])
        cr, ci = lr_[-1], li_[-1]
        for idx, k in enumerate(S5_HS_STEPS):
            qr, qi = pw(PW_HS + idx)
            mr, mi = _cmul(_shift_rows(cr, k, row), _shift_rows(ci, k, row), qr, qi)
            cr, ci = cr + mr, ci + mi
        cin_r = carry_ref[j, 0:1, :]
        cin_i = carry_ref[j, 1:2, :]
        tr = pw_ref[j, 0, PW_CARRY:PW_CARRY + J, :]
        ti = pw_ref[j, 1, PW_CARRY:PW_CARRY + J, :]
        mr, mi = _cmul(tr, ti, cin_r, cin_i)
        cr, ci = cr + mr, ci + mi
        carry_ref[j, 0:1, :] = cr[J - 1:J, :]
        carry_ref[j, 1:2, :] = ci[J - 1:J, :]
        cpr = jnp.where(row >= 1, pltpu.roll(cr, 1, 0), cin_r)
        cpi = jnp.where(row >= 1, pltpu.roll(ci, 1, 0), cin_i)
        sp_r, sp_i = [cpr], [cpi]
        for ph in range(1, S5_PH):
            qr, qi = pw(PW_L1 + ph - 1)
            mr, mi = _cmul(cpr, cpi, qr, qi)
            sp_r.append(lr_[ph - 1] + mr)
            sp_i.append(li_[ph - 1] + mi)
        sprev = jnp.concatenate([jnp.concatenate(sp_r, axis=0), jnp.concatenate(sp_i, axis=0)], axis=1)
        y = y_loc + _dot(_bf(sprev), f_ref[j])
        for ph in range(S5_PH):
            for t in range(S5_R):
                y_s[j, pl.ds(S5_R * ph + t, J, stride=span), :] = y[ph * J:(ph + 1) * J, t * LANES:(t + 1) * LANES]
    tick(2)
    yb = jnp.concatenate(
        [jnp.concatenate([y_s[j, sb * S5_PITCH:sb * S5_PITCH + S5_SPAN, :] for sb in range(S5_J)], axis=0)
         for j in range(S5_SLABS)], axis=1) + d_ref[...] * pb_ref[:, 0:D_BRANCH]
    yb = _gelu_tanh(yb)
    glu = _dot(_bf(yb), gw_ref[...])
    tick(2)
    yb = yb * _sigmoid(glu + gb_ref[...])
    ycat_ref[:, D_BRANCH:2 * D_BRANCH] = _bf(yb * _silu(pb_ref[:, D_BRANCH:2 * D_BRANCH]))


def _mixer_gla(pc_ref, ycat_ref, qd_s, ki_s, ke_s, gl_s, o_s, st_s, wg_ref, bg_ref, nw_ref, tick):
    k = pc_ref[:, GLA_DK:2 * GLA_DK]
    z_off = 2 * GLA_DK + D_BRANCH
    glow = pc_ref[:, 2 * GLA_DK + 2 * D_BRANCH:SEG_C_W]
    logits = _dot(_bf(glow), wg_ref[...]) + bg_ref[...]
    tick()
    g = -_softplus(-logits) / GLA_TAU
    gc = _chunk_cumsum(g)
    tick(2)
    qd_s[...] = _bf(pc_ref[:, 0:GLA_DK] * (GLA_HK ** -0.5) * jnp.exp(gc))
    ki_s[...] = _bf(k * jnp.exp(-gc))
    gl = jnp.concatenate(
        [jnp.broadcast_to(gc[c * CHUNK + CHUNK - 1:c * CHUNK + CHUNK, :], (CHUNK, GLA_DK)) for c in range(NCHUNK)], axis=0)
    ke_s[...] = _bf(k * jnp.exp(gl - gc))
    for c in range(NCHUNK):
        gl_s[c * SUBLANES:(c + 1) * SUBLANES, :] = jnp.exp(gl[c * CHUNK:c * CHUNK + SUBLANES, :])

    lane = _lane_iota((1, LANES))
    m_lo = (lane < GLA_HK).astype(BF16)
    m_hi = (lane >= GLA_HK).astype(BF16)
    tri = (_lane_iota((CHUNK, CHUNK)) <= _row_iota((CHUNK, CHUNK)))

    for c in range(NCHUNK):
        if c % 2 == 0:
            tick()
        r0 = c * CHUNK
        rows = pl.ds(r0, CHUNK)
        qst, kes, decs, s2s = [], [], [], []
        for tl in range(GLA_DK // LANES):
            cols = slice(tl * LANES, (tl + 1) * LANES)
            qd = qd_s[rows, cols]
            kes.append(ke_s[rows, cols])
            decs.append(gl_s[pl.ds(c * SUBLANES, 1), cols])
            qstack = jnp.concatenate([qd * m_lo, qd * m_hi], axis=0)
            qst.append(qstack)
            s2s.append(_dot_nt(qstack, ki_s[rows, cols]))
        heads = range(GLA_HEADS)
        half = lambda h: slice((h % 2) * CHUNK, (h % 2 + 1) * CHUNK)
        vhs = [_bf(pc_ref[rows, 2 * GLA_DK + h * GLA_HV:2 * GLA_DK + (h + 1) * GLA_HV]) for h in heads]
        sts = [st_s[h] for h in heads]
        o_inter = [_dot_nt(qst[h // 2][half(h), :], _bf(sts[h])) for h in heads]
        kvs = [_dot_tn(vhs[h], kes[h // 2]) for h in heads]
        o_intra = [_dot(_bf(jnp.where(tri, s2s[h // 2][half(h), :], 0.0)), vhs[h]) for h in heads]
        for h in heads:
            o_s[h, rows, :] = o_intra[h] + o_inter[h]
            st_s[h] = sts[h] * decs[h // 2] + kvs[h]

    tick()
    for h in range(GLA_HEADS):
        o = o_s[h, 0:TILE, :]
        o = o * lax.rsqrt(jnp.mean(o * o, axis=-1, keepdims=True) + EPS) * nw_ref[...]
        zh = pc_ref[:, z_off + h * GLA_HV:z_off + (h + 1) * GLA_HV]
        ycat_ref[:, 2 * D_BRANCH + h * GLA_HV:2 * D_BRANCH + (h + 1) * GLA_HV] = _bf(o * _silu(zh))


def _mixer_ssd(pd_ref, ycat_ref, xbc_s, ac_s, dt_s, y_s, st_s,
               cw_ref, cb_ref, dtb_ref, alog_ref, dsk_ref, nw_ref, tick):
    zt = D_BRANCH // LANES
    xt = 2 * D_BRANCH // LANES
    body = slice(SUBLANES, SUBLANES + TILE)
    tick(2)
    for k in range(xt):
        for r, ur in enumerate(_phase_conv(pd_ref, zt + k, cw_ref, cb_ref, k * LANES)):
            xbc_s[k, pl.ds(r, NPH, stride=CONV_PH), :] = _silu(ur)
    dt = _softplus(pd_ref[zt + xt, body, :] + dtb_ref[...])
    a = -jnp.exp(alog_ref[...])
    dt_s[...] = dt
    ac_s[...] = _chunk_cumsum(dt * a)
    tick()

    lane = _lane_iota((1, LANES))
    lo = lane < SSD_HDIM
    m_lo = lo.astype(F32)
    m_hi = 1.0 - m_lo
    lane_j = jnp.where(lo, lane, lane - SSD_HDIM)
    tri2 = lane_j <= _row_iota((CHUNK, LANES))
    npairs = SSD_HEADS // 2
    ppg = npairs // SSD_GROUPS

    for c in range(NCHUNK):
        tick()
        r0 = c * CHUNK
        rows = pl.ds(r0, CHUNK)
        ac = ac_s[rows, :]
        dtc = dt_s[rows, :]
        a_last = ac_s[pl.ds(r0 + CHUNK - 1, 1), :]
        dend = jnp.exp(a_last - ac) * dtc
        cdec = jnp.exp(a_last)
        act = jnp.concatenate([ac, ac], axis=0).T
        dtt = jnp.concatenate([dtc, dtc], axis=0).T
        for g in range(SSD_GROUPS):
            bm = xbc_s[npairs + g, rows, :]
            cm = xbc_s[npairs + SSD_GROUPS + g, rows, :]
            bmb = _bf(bm)
            cb2 = _dot_nt(_bf(cm), jnp.concatenate([bmb, bmb], axis=0))
            st = st_s[g]
            y_off = _dot(_bf(cm), _bf(st))
            xd, cdrow = [], []
            for pp in range(ppg):
                pr = g * ppg + pp
                h0, h1 = 2 * pr, 2 * pr + 1
                xp = xbc_s[pr, rows, :]
                acol = jnp.where(lo, jnp.broadcast_to(ac[:, h0:h0 + 1], (CHUNK, LANES)),
                                 jnp.broadcast_to(ac[:, h1:h1 + 1], (CHUNK, LANES)))
                arow = jnp.where(lo, act[h0:h0 + 1, :], act[h1:h1 + 1, :])
                dtrow = jnp.where(lo, dtt[h0:h0 + 1, :], dtt[h1:h1 + 1, :])
                seg = jnp.where(tri2, jnp.exp(jnp.where(tri2, acol - arow, 0.0)), 0.0)
                m = cb2 * seg * dtrow
                xbd = jnp.concatenate([xp * m_lo, xp * m_hi], axis=0)
                y = _dot(_bf(m), _bf(xbd))
                y = y + y_off[:, pp * LANES:(pp + 1) * LANES] * jnp.exp(acol)
                y = y + dsk_ref[:, pr * LANES:(pr + 1) * LANES] * xp
                y_s[pr, rows, :] = y
                dcol = jnp.where(lo, jnp.broadcast_to(dend[:, h0:h0 + 1], (CHUNK, LANES)),
                                 jnp.broadcast_to(dend[:, h1:h1 + 1], (CHUNK, LANES)))
                xd.append(xp * dcol)
                cdrow.append(jnp.where(lo, jnp.broadcast_to(cdec[:, h0:h0 + 1], (1, LANES)),
                                       jnp.broadcast_to(cdec[:, h1:h1 + 1], (1, LANES))))
            snew = _dot_tn(bmb, _bf(jnp.concatenate(xd, axis=1)))
            st_s[g] = st * jnp.concatenate(cdrow, axis=1) + snew

    z = jnp.concatenate([pd_ref[k, body, :] for k in range(zt)], axis=1)
    y = jnp.concatenate([y_s[pr, 0:TILE, :] for pr in range(npairs)], axis=1) * _silu(z)
    _keep_tail(pd_ref)
    y = y * lax.rsqrt(jnp.mean(y * y, axis=-1, keepdims=True) + EPS) * nw_ref[...]
    ycat_ref[:, 3 * D_BRANCH:4 * D_BRANCH] = _bf(y)


class _Interleave:
    def __init__(self):
        self.queue = []

    def load(self, thunks):
        self.flush()
        self.queue = list(thunks)

    def tick(self, n=1):
        for _ in range(n):
            if self.queue:
                self.queue.pop(0)()

    def flush(self):
        while self.queue:
            self.queue.pop(0)()


PROJ_CHUNK = 256


def _layer_kernel(final, tiles_per_seq,
                  x_ref, nw_ref, win_ref, wind_ref,
                  a_cw, a_cb, a_wg, a_bg, a_l,
                  b_toep, b_eb, b_f, b_pw, b_d, b_gw, b_gb,
                  c_wg, c_bg, c_nw,
                  d_cw, d_cb, d_dtb, d_alog, d_dsk, d_nw,
                  wout_ref, nf_ref,
                  o_ref,
                  pa_s, pb_s, pc_s, pd_s, xs_s, hb_s,
                  ycat, buf0, buf1, lru_carry, s5_carry,
                  qd_s, ki_s, ke_s, gl_s, gla_st, xbc_s, ac_s, dt_s, ssd_st):
    g = pl.program_id(0)

    @pl.when(g == 0)
    def _():
        pa_s[...] = jnp.zeros_like(pa_s)
        pb_s[...] = jnp.zeros_like(pb_s)
        pc_s[...] = jnp.zeros_like(pc_s)
        pd_s[...] = jnp.zeros_like(pd_s)
        xs_s[...] = jnp.zeros_like(xs_s)
        hb_s[...] = jnp.zeros_like(hb_s)

    @pl.when((g == 0) | (lax.rem(g - 1, tiles_per_seq) == 0))
    def _():
        pa_s[:, 0:SUBLANES, :] = jnp.zeros((pa_s.shape[0], SUBLANES, LANES), F32)
        pd_s[:, 0:SUBLANES, :] = jnp.zeros((pd_s.shape[0], SUBLANES, LANES), F32)
        lru_carry[...] = jnp.zeros_like(lru_carry)
        s5_carry[...] = jnp.zeros_like(s5_carry)
        gla_st[...] = jnp.zeros_like(gla_st)
        ssd_st[...] = jnp.zeros_like(ssd_st)

    def proj_chunks(w_ref, w_off, dst_ref, width, tiled):
        out = []
        for c0 in range(0, width, PROJ_CHUNK):
            c1 = min(c0 + PROJ_CHUNK, width)

            def thunk(c0=c0, c1=c1):
                res = _dot(hb_s[...], w_ref[:, w_off + c0:w_off + c1])
                if tiled:
                    for i in range((c1 - c0) // LANES):
                        dst_ref[c0 // LANES + i, SUBLANES:SUBLANES + TILE, :] = res[:, i * LANES:(i + 1) * LANES]
                else:
                    dst_ref[:, c0:c1] = res
            out.append(thunk)
        return out

    def out_chunks(m):
        out = []
        rows = slice(m * D_BRANCH, (m + 1) * D_BRANCH)
        for c0 in range(0, D_MODEL, PROJ_CHUNK):
            cols = slice(c0, c0 + PROJ_CHUNK)

            def thunk(cols=cols):
                base = xs_s[:, cols] if m == 0 else o_ref[:, cols]
                o_ref[:, cols] = base + _dot(ycat[:, rows], wout_ref[rows, cols])
            out.append(thunk)
        return out

    il = _Interleave()
    il.load(proj_chunks(wind_ref, 0, pd_s, SEG_D_W, True))
    ntile = D_BRANCH // LANES
    p2_s, l2_s = _SubScratch(xbc_s, 0, 0), _SubScratch(xbc_s, ntile, 0)
    c2_s = _SubScratch(xbc_s, 0, TILE // 2)
    _mixer_lru(pa_s, ycat, buf0, buf1, p2_s, l2_s, c2_s, lru_carry, a_cw, a_cb, a_wg, a_bg, a_l, il.tick)
    il.flush()
    hb_s[...] = _bf(_rmsnorm(x_ref[...], nw_ref[...]))
    il.load(proj_chunks(win_ref, SEG_A, pa_s, SEG_B - SEG_A, True) + out_chunks(0))
    _mixer_s5(pb_s, ycat, buf0, buf1, s5_carry, b_toep, b_eb, b_f, b_pw, b_d, b_gw, b_gb, il.tick)
    il.load(proj_chunks(win_ref, SEG_B, pb_s, SEG_C - SEG_B, False) + out_chunks(1))
    _mixer_gla(pc_s, ycat, qd_s, ki_s, ke_s, gl_s, buf1, gla_st, c_wg, c_bg, c_nw, il.tick)
    il.load(proj_chunks(win_ref, SEG_C, pc_s, SEG_C_W, False) + out_chunks(2))
    _mixer_ssd(pd_s, ycat, xbc_s, ac_s, dt_s, buf1, ssd_st, d_cw, d_cb, d_dtb, d_alog, d_dsk, d_nw, il.tick)
    il.load(out_chunks(3))
    il.flush()
    if final:
        o_ref[...] = _rmsnorm(o_ref[...], nf_ref[...])
    xs_s[...] = x_ref[...]


def _layer_call(layer, final, x, consts):
    bsz, seq, _ = x.shape
    assert seq % TILE == 0
    tiles_per_seq = seq // TILE
    ntiles = bsz * tiles_per_seq

    def wspec(arr):
        nd = arr.ndim - 1
        return pl.BlockSpec((None,) + arr.shape[1:], lambda g, _n=nd: (layer,) + (0,) * _n,
                            pipeline_mode=pl.Buffered(1))

    def cspec(arr):
        nd = arr.ndim
        return pl.BlockSpec(arr.shape, lambda g, _n=nd: (0,) * _n, pipeline_mode=pl.Buffered(1))

    def tile_index(t):
        return (t // tiles_per_seq, t % tiles_per_seq, 0)

    per_layer = consts["per_layer"]
    shared = consts["shared"]
    in_x = pl.BlockSpec((None, TILE, D_MODEL), lambda g: tile_index(jnp.minimum(g, ntiles - 1)))
    out_x = pl.BlockSpec((None, TILE, D_MODEL), lambda g: tile_index(jnp.maximum(g - 1, 0)))
    scratch = [
        pltpu.VMEM(((SEG_B - SEG_A) // LANES, SUBLANES + TILE, LANES), F32),
        pltpu.VMEM((TILE, SEG_C - SEG_B), F32),
        pltpu.VMEM((TILE, SEG_C_W), F32),
        pltpu.VMEM((SEG_D_W // LANES, SUBLANES + TILE, LANES), F32),
        pltpu.VMEM((TILE, D_MODEL), F32),
        pltpu.VMEM((TILE, D_MODEL), BF16),
        pltpu.VMEM((TILE, 4 * D_BRANCH), BF16),
        pltpu.VMEM((4, S5_ROWS, LANES), F32),
        pltpu.VMEM((4, S5_ROWS, LANES), F32),
        pltpu.VMEM((SUBLANES, D_BRANCH), F32),
        pltpu.VMEM((S5_SLABS, SUBLANES, S5_SW), F32),
        pltpu.VMEM((TILE, GLA_DK), BF16),
        pltpu.VMEM((TILE, GLA_DK), BF16),
        pltpu.VMEM((TILE, GLA_DK), BF16),
        pltpu.VMEM((NCHUNK * SUBLANES, GLA_DK), F32),
        pltpu.VMEM((GLA_HEADS, GLA_HV, LANES), F32),
        pltpu.VMEM((2 * D_BRANCH // LANES, TILE, LANES), F32),
        pltpu.VMEM((TILE, LANES), F32),
        pltpu.VMEM((TILE, LANES), F32),
        pltpu.VMEM((SSD_GROUPS, SSD_STATE, D_BRANCH // SSD_GROUPS), F32),
    ]
    return pl.pallas_call(
        functools.partial(_layer_kernel, final, tiles_per_seq),
        grid=(ntiles + 1,),
        in_specs=[in_x] + [wspec(a) for a in per_layer] + [cspec(a) for a in shared],
        out_specs=out_x,
        out_shape=jax.ShapeDtypeStruct(x.shape, F32),
        scratch_shapes=scratch,
        compiler_params=pltpu.CompilerParams(
            dimension_semantics=("arbitrary",),
            vmem_limit_bytes=VMEM_LIMIT_BYTES),
        name=f"layer{layer}",
    )(x, *per_layer, *shared)


def _prepare(norm_w, w_in, lru_conv_w, lru_conv_b, lru_w_r, lru_b_r, lru_w_i, lru_b_i, lru_l,
             s5_lam_re, s5_lam_im, s5_log_dt, s5_b_re, s5_b_im, s5_c_re, s5_c_im, s5_d, s5_glu_w, s5_glu_b,
             gla_w_gate, gla_b_gate, gla_norm_w,
             ssd_conv_w, ssd_conv_b, ssd_dt_bias, ssd_a_log, ssd_d, ssd_norm_w,
             w_out, norm_f_w):
    nl = w_in.shape[0]
    row = lambda a: a.reshape(nl, 1, -1).astype(F32)
    w_abc = jnp.pad(w_in[:, :, :ORIG_C_END], ((0, 0), (0, 0), (0, SEG_D - ORIG_C_END))).astype(BF16)
    w_d = jnp.pad(w_in[:, :, ORIG_C_END:], ((0, 0), (0, 0), (0, SEG_D_W - (ORIG_D_IN - ORIG_C_END)))).astype(BF16)
    hph = LRU_HEADS // 2
    eye = jnp.eye(hph, dtype=F32)

    def bd(w):
        w = w.reshape(nl, 2, hph, LRU_HDIM, LRU_HDIM)
        return jnp.einsum("lfhij,hk->lfhikj", w, eye).reshape(nl, 2, hph * LRU_HDIM, hph * LRU_HDIM)

    a_wg = jnp.concatenate([bd(lru_w_r), bd(lru_w_i)], axis=-1).astype(BF16)
    half = D_BRANCH // 2
    br = lru_b_r.reshape(nl, 2, half)
    bi = lru_b_i.reshape(nl, 2, half)
    a_bg = jnp.concatenate([br, bi], axis=-1).reshape(nl, 1, 2 * D_BRANCH)
    toep, ebm, fm, pw = _s5_prepare(s5_lam_re, s5_lam_im, s5_log_dt, s5_b_re, s5_b_im, s5_c_re, s5_c_im)
    c_wg = jnp.concatenate([gla_w_gate, jnp.zeros((nl, LANES - GLA_RANK, GLA_DK), gla_w_gate.dtype)], axis=1).astype(BF16)
    pad_h = lambda a: jnp.concatenate([a, jnp.zeros((nl, LANES - SSD_HEADS), a.dtype)], axis=-1).reshape(nl, 1, LANES)
    per_layer = [
        row(norm_w), w_abc, w_d,
        lru_conv_w.astype(F32), row(lru_conv_b), a_wg, a_bg, row(lru_l),
        toep, ebm, fm, pw, row(s5_d), s5_glu_w.astype(BF16), row(s5_glu_b),
        c_wg, row(gla_b_gate), row(gla_norm_w),
        ssd_conv_w.astype(F32), row(ssd_conv_b), pad_h(ssd_dt_bias), pad_h(ssd_a_log),
        row(jnp.repeat(ssd_d, SSD_HDIM, axis=-1)), row(ssd_norm_w),
        w_out.astype(BF16),
    ]
    shared = [norm_f_w.reshape(1, D_MODEL).astype(F32)]
    return {"per_layer": per_layer, "shared": shared}


def kernel(x, norm_w, w_in, lru_conv_w, lru_conv_b, lru_w_r, lru_b_r, lru_w_i, lru_b_i, lru_l, s5_lam_re, s5_lam_im, s5_log_dt, s5_b_re, s5_b_im, s5_c_re, s5_c_im, s5_d, s5_glu_w, s5_glu_b, gla_w_gate, gla_b_gate, gla_norm_w, ssd_conv_w, ssd_conv_b, ssd_dt_bias, ssd_a_log, ssd_d, ssd_norm_w, w_out, norm_f_w):
    consts = _prepare(norm_w, w_in, lru_conv_w, lru_conv_b, lru_w_r, lru_b_r, lru_w_i, lru_b_i, lru_l,
                      s5_lam_re, s5_lam_im, s5_log_dt, s5_b_re, s5_b_im, s5_c_re, s5_c_im, s5_d, s5_glu_w, s5_glu_b,
                      gla_w_gate, gla_b_gate, gla_norm_w,
                      ssd_conv_w, ssd_conv_b, ssd_dt_bias, ssd_a_log, ssd_d, ssd_norm_w,
                      w_out, norm_f_w)
    nl = w_in.shape[0]
    for layer in range(nl):
        x = _layer_call(layer, layer == nl - 1, x, consts)
    return x
```
